```python
import jax, jax.numpy as jnp
from jax import lax
import numpy as np

D_MODEL = 2048
BATCH = 16
SEQ = 2048
DEPTH = 1

D_MIX = D_MODEL
D_ATTN = D_MIX // 2
D_SSM = D_MIX - D_ATTN
ATTN_HEAD_DIM = 128
N_ATTN_HEADS = D_ATTN // ATTN_HEAD_DIM
ROT_DIM = ATTN_HEAD_DIM // 4
ROPE_THETA = 500000.0
DILATED_PATTERNS = ((128, 1), (512, 4), (2048, 16))
ATTN_BLOCK = 128
SSM_HEAD_DIM = 64
N_SSM_HEADS = D_SSM // SSM_HEAD_DIM
SSM_GROUPS = 8
SSM_STATE = 128
CONV_WIDTH = 4
SSD_CHUNK = 128
D_CONV_CH = D_SSM + 2 * SSM_GROUPS * SSM_STATE
D_IN_PROJ = 3 * D_ATTN + D_SSM + D_CONV_CH + N_SSM_HEADS
N_EXPERT_GROUPS = 8
EXPERTS_PER_GROUP = 8
N_EXPERTS = N_EXPERT_GROUPS * EXPERTS_PER_GROUP
TOP_K_INNER = 2
D_EXPERT = D_MODEL // 4
MOE_BLOCK = 128
EPS = 1e-6

kernel_name = 'hymba_dilated_ssd_hmoe_block'


def rms_norm(x, w):
    xf = x.astype(jnp.float32)
    y = xf * lax.rsqrt(jnp.mean(xf * xf, axis=-1, keepdims=True) + EPS)
    return (y * w.astype(jnp.float32)).astype(x.dtype)


def rope_tables(positions):
    inv_freq = ROPE_THETA ** (-(jnp.arange(0, ROT_DIM, 2, dtype=jnp.float32) / ROT_DIM))
    ang = positions.astype(jnp.float32)[..., None] * inv_freq
    return jnp.cos(ang)[:, :, None, :], jnp.sin(ang)[:, :, None, :]


def partial_rope(t, cos, sin):
    half = ROT_DIM // 2
    tr = t[..., :ROT_DIM].astype(jnp.float32)
    t1, t2 = tr[..., :half], tr[..., half:]
    rot = jnp.concatenate([t1 * cos - t2 * sin, t2 * cos + t1 * sin], axis=-1)
    return jnp.concatenate([rot.astype(t.dtype), t[..., ROT_DIM:]], axis=-1)


def dilated_window_attention(q, k, v, window, dilation):
    b, s, h, e = q.shape
    span = window // dilation
    assert span <= ATTN_BLOCK
    sub_len = s // dilation
    n_blk = -(-sub_len // ATTN_BLOCK)
    pad = n_blk * ATTN_BLOCK - sub_len

    def gather_stride(t):
        t = t.reshape(b, sub_len, dilation, h, e).transpose(0, 2, 1, 3, 4)
        t = jnp.pad(t, ((0, 0), (0, 0), (0, pad), (0, 0), (0, 0)))
        return t.reshape(b, dilation, n_blk, ATTN_BLOCK, h, e)

    def with_prev_block(t):
        prev = jnp.pad(t, ((0, 0), (0, 0), (1, 0), (0, 0), (0, 0), (0, 0)))[:, :, :-1]
        return jnp.concatenate([prev, t], axis=3)

    qb = gather_stride(q)
    kb = with_prev_block(gather_stride(k))
    vb = with_prev_block(gather_stride(v))
    scores = jnp.einsum('bdnqhe,bdnkhe->bdnhqk', qb, kb,
                        preferred_element_type=jnp.float32) * (e ** -0.5)
    qi = jnp.arange(ATTN_BLOCK)[:, None]
    kj = jnp.arange(2 * ATTN_BLOCK)[None, :]
    dist = ATTN_BLOCK + qi - kj
    blk = jnp.arange(n_blk)[:, None, None]
    valid = (dist >= 0) & (dist <= span) & ((blk > 0) | (kj >= ATTN_BLOCK))
    scores = jnp.where(valid[None, None, :, None], scores, -jnp.inf)
    m = jnp.max(scores, axis=-1, keepdims=True)
    p = jnp.exp(scores - m)
    den = jnp.sum(p, axis=-1, keepdims=True)
    inv_den = jnp.swapaxes(1.0 / den[..., 0], -1, -2)[..., None]
    out = jnp.einsum('bdnhqk,bdnkhe->bdnqhe', p, vb.astype(jnp.float32)) * inv_den
    lse = jnp.swapaxes((m + jnp.log(den))[..., 0], -1, -2)
    out = out.reshape(b, dilation, n_blk * ATTN_BLOCK, h, e)[:, :, :sub_len]
    out = out.transpose(0, 2, 1, 3, 4).reshape(b, s, h, e)
    lse = lse.reshape(b, dilation, n_blk * ATTN_BLOCK, h)[:, :, :sub_len]
    lse = lse.transpose(0, 2, 1, 3).reshape(b, s, h)
    return out, lse


def dilated_mixture_attention(q, k, v):
    outs, lses = [], []
    for window, dilation in DILATED_PATTERNS:
        o, l = dilated_window_attention(q, k, v, window, dilation)
        outs.append(o)
        lses.append(l)
    wts = jax.nn.softmax(jnp.stack(lses), axis=0)
    return jnp.sum(wts[..., None] * jnp.stack(outs), axis=0)


def causal_depthwise_conv(x, w, bias):
    ch = x.shape[-1]
    y = lax.conv_general_dilated(x, w[:, None, :].astype(x.dtype), window_strides=(1,),
                                 padding=((CONV_WIDTH - 1, 0),),
                                 dimension_numbers=('NWC', 'WIO', 'NWC'),
                                 feature_group_count=ch)
    return y + bias.astype(x.dtype)


def ssd_chunked(xs, dt, a, bm, cm):
    b, s, h, p = xs.shape
    g, n = bm.shape[2], bm.shape[3]
    r = h // g
    nc = s // SSD_CHUNK
    xdt = (xs * dt[..., None]).reshape(b, nc, SSD_CHUNK, g, r, p)
    acs = jnp.cumsum((dt * a).reshape(b, nc, SSD_CHUNK, g, r), axis=2)
    bc = bm.reshape(b, nc, SSD_CHUNK, g, n)
    cc = cm.reshape(b, nc, SSD_CHUNK, g, n)
    causal = jnp.tril(jnp.ones((SSD_CHUNK, SSD_CHUNK), dtype=bool))
    seg = acs[:, :, :, None] - acs[:, :, None, :]
    decay_in = jnp.exp(jnp.where(causal[:, :, None, None], seg, -jnp.inf))
    cb = jnp.einsum('bclgn,bcsgn->bclsg', cc, bc)
    y_diag = jnp.einsum('bclsg,bclsgr,bcsgrp->bclgrp', cb, decay_in, xdt)
    decay_to_end = jnp.exp(acs[:, :, -1:] - acs)
    states = jnp.einsum('bclgn,bclgr,bclgrp->bcgrpn', bc, decay_to_end, xdt)
    chunk_decay = jnp.exp(acs[:, :, -1])

    def step(carry, inp):
        st, dec = inp
        return carry * dec[..., None, None] + st, carry

    _, prev = lax.scan(step, jnp.zeros_like(states[:, 0]),
                       (jnp.moveaxis(states, 1, 0), jnp.moveaxis(chunk_decay, 1, 0)))
    prev = jnp.moveaxis(prev, 0, 1)
    y_off = jnp.einsum('bclgn,bcgrpn,bclgr->bclgrp', cc, prev, jnp.exp(acs))
    return (y_diag + y_off).reshape(b, s, h, p)


def hybrid_mixer(h, cos, sin, w_in, q_norm_w, k_norm_w, conv_w, conv_b, dt_bias, a_log,
                 d_skip, ssm_norm_w, w_out):
    b, s, _ = h.shape
    proj = jnp.einsum('bsd,de->bse', h, w_in)
    splits = [D_ATTN, 2 * D_ATTN, 3 * D_ATTN, 3 * D_ATTN + D_SSM,
              3 * D_ATTN + D_SSM + D_CONV_CH]
    q, k, v, z, xbc, dt_raw = jnp.split(proj, splits, axis=-1)
    hs = (b, s, N_ATTN_HEADS, ATTN_HEAD_DIM)
    q = partial_rope(rms_norm(q.reshape(hs), q_norm_w), cos, sin)
    k = partial_rope(rms_norm(k.reshape(hs), k_norm_w), cos, sin)
    y_attn = dilated_mixture_attention(q, k, v.reshape(hs)).reshape(b, s, D_ATTN).astype(h.dtype)
    xbc = jax.nn.silu(causal_depthwise_conv(xbc, conv_w, conv_b))
    xs, bm, cm = jnp.split(xbc, [D_SSM, D_SSM + SSM_GROUPS * SSM_STATE], axis=-1)
    dt = jax.nn.softplus(dt_raw.astype(jnp.float32) + dt_bias.astype(jnp.float32))
    a = -jnp.exp(a_log.astype(jnp.float32))
    xs = xs.reshape(b, s, N_SSM_HEADS, SSM_HEAD_DIM).astype(jnp.float32)
    y = ssd_chunked(xs, dt, a,
                    bm.reshape(b, s, SSM_GROUPS, SSM_STATE).astype(jnp.float32),
                    cm.reshape(b, s, SSM_GROUPS, SSM_STATE).astype(jnp.float32))
    y = y + d_skip.astype(jnp.float32)[:, None] * xs
    y = y.reshape(b, s, D_SSM) * jax.nn.silu(z.astype(jnp.float32))
    y = rms_norm(y.reshape(b, s, SSM_GROUPS, D_SSM // SSM_GROUPS),
                 ssm_norm_w.reshape(SSM_GROUPS, D_SSM // SSM_GROUPS))
    y_ssm = y.reshape(b, s, D_SSM).astype(h.dtype)
    return jnp.einsum('bse,ed->bsd', jnp.concatenate([y_attn, y_ssm], axis=-1), w_out)


def hierarchical_moe(h, router_group_w, router_group_b, router_expert_w, router_expert_b,
                     w_gate, w_up, w_down):
    b, s, d = h.shape
    t = b * s
    ht = h.reshape(t, d)
    g_logits = (ht @ router_group_w + router_group_b).astype(jnp.float32)
    g_prob = jax.nn.softmax(g_logits, axis=-1)
    g_gate, g_idx = lax.top_k(g_prob, 1)
    e_logits = (ht @ router_expert_w + router_expert_b).astype(jnp.float32)
    e_logits = e_logits.reshape(t, N_EXPERT_GROUPS, EXPERTS_PER_GROUP)
    e_in = jnp.take_along_axis(e_logits, g_idx[:, :, None], axis=1)[:, 0]
    e_top, e_local = lax.top_k(e_in, TOP_K_INNER)
    e_w = jax.nn.softmax(e_top, axis=-1) * g_gate
    expert_id = g_idx * EXPERTS_PER_GROUP + e_local
    n_assign = t * TOP_K_INNER
    flat_e = expert_id.reshape(-1)
    flat_tok = jnp.arange(n_assign, dtype=jnp.int32) // TOP_K_INNER
    order = jnp.argsort(flat_e)
    se, stok, sw = flat_e[order], flat_tok[order], e_w.reshape(-1)[order]
    counts = jnp.bincount(flat_e, length=N_EXPERTS)
    start = jnp.cumsum(counts) - counts
    padded = (counts + MOE_BLOCK - 1) // MOE_BLOCK * MOE_BLOCK
    pend = jnp.cumsum(padded)
    pstart = pend - padded
    dest = pstart[se] + (jnp.arange(n_assign) - start[se])
    n_blocks = -(-(n_assign + N_EXPERTS * (MOE_BLOCK - 1)) // MOE_BLOCK)
    rows = n_blocks * MOE_BLOCK
    xd = jnp.zeros((rows, d), h.dtype).at[dest].set(ht[stok])
    block_e = jnp.minimum(jnp.searchsorted(pend, jnp.arange(n_blocks) * MOE_BLOCK, side='right'),
                          N_EXPERTS - 1)

    def expert_block(args):
        xb, e = args
        hid = jax.nn.silu(xb @ w_gate[e]) * (xb @ w_up[e])
        return hid @ w_down[e]

    yd = lax.map(expert_block, (xd.reshape(n_blocks, MOE_BLOCK, d), block_e)).reshape(rows, d)
    y_assign = yd[dest].astype(jnp.float32) * sw[:, None]
    out = jax.ops.segment_sum(y_assign, stok, num_segments=t)
    return out.reshape(b, s, d).astype(h.dtype)


def setup_inputs(seed: int = 0) -> dict:
    key = jax.random.key(seed)
    ks = jax.random.split(key, 24)
    f32 = jnp.float32

    def nrm(k, shape, scale):
        return jax.random.normal(k, shape, f32) * scale

    x = nrm(ks[0], (BATCH, SEQ, D_MODEL), 1.0)
    c = nrm(ks[1], (BATCH, D_MODEL), 1.0)
    positions = (jax.random.randint(ks[2], (BATCH, 1), 0, 1024, dtype=jnp.int32)
                 + jnp.arange(SEQ, dtype=jnp.int32)[None, :])
    ada_w = nrm(ks[3], (DEPTH, D_MODEL, 6 * D_MODEL), 0.5 * D_MODEL ** -0.5)
    ada_b = nrm(ks[4], (DEPTH, 6 * D_MODEL), 0.02)
    norm1_w = 1.0 + nrm(ks[5], (DEPTH, D_MODEL), 0.05)
    w_in = nrm(ks[6], (DEPTH, D_MODEL, D_IN_PROJ), D_MODEL ** -0.5)
    q_norm_w = 1.0 + nrm(ks[7], (DEPTH, ATTN_HEAD_DIM), 0.05)
    k_norm_w = 1.0 + nrm(ks[8], (DEPTH, ATTN_HEAD_DIM), 0.05)
    conv_w = nrm(ks[9], (DEPTH, CONV_WIDTH, D_CONV_CH), CONV_WIDTH ** -0.5)
    conv_b = nrm(ks[10], (DEPTH, D_CONV_CH), 0.02)
    dt0 = jnp.exp(jax.random.uniform(ks[11], (DEPTH, N_SSM_HEADS), f32,
                                     float(np.log(1e-3)), float(np.log(1e-1))))
    dt_bias = dt0 + jnp.log(-jnp.expm1(-dt0))
    a_log = jnp.log(jax.random.uniform(ks[12], (DEPTH, N_SSM_HEADS), f32, 1.0, 16.0))
    d_skip = 1.0 + nrm(ks[13], (DEPTH, N_SSM_HEADS), 0.1)
    ssm_norm_w = 1.0 + nrm(ks[14], (DEPTH, D_SSM), 0.05)
    w_out = nrm(ks[15], (DEPTH, D_MIX, D_MODEL), D_MIX ** -0.5)
    norm2_w = 1.0 + nrm(ks[16], (DEPTH, D_MODEL), 0.05)
    router_group_w = nrm(ks[17], (DEPTH, D_MODEL, N_EXPERT_GROUPS), D_MODEL ** -0.5)
    router_group_b = nrm(ks[18], (DEPTH, N_EXPERT_GROUPS), 0.01)
    router_expert_w = nrm(ks[19], (DEPTH, D_MODEL, N_EXPERTS), D_MODEL ** -0.5)
    router_expert_b = nrm(ks[20], (DEPTH, N_EXPERTS), 0.01)
    w_gate = nrm(ks[21], (DEPTH, N_EXPERTS, D_MODEL, D_EXPERT), D_MODEL ** -0.5)
    w_up = nrm(ks[22], (DEPTH, N_EXPERTS, D_MODEL, D_EXPERT), D_MODEL ** -0.5)
    w_down = nrm(ks[23], (DEPTH, N_EXPERTS, D_EXPERT, D_MODEL), D_EXPERT ** -0.5)
    return {'x': x, 'c': c, 'positions': positions, 'ada_w': ada_w, 'ada_b': ada_b,
            'norm1_w': norm1_w, 'w_in': w_in, 'q_norm_w': q_norm_w, 'k_norm_w': k_norm_w,
            'conv_w': conv_w, 'conv_b': conv_b, 'dt_bias': dt_bias, 'a_log': a_log,
            'd_skip': d_skip, 'ssm_norm_w': ssm_norm_w, 'w_out': w_out, 'norm2_w': norm2_w,
            'router_group_w': router_group_w, 'router_group_b': router_group_b,
            'router_expert_w': router_expert_w, 'router_expert_b': router_expert_b,
            'w_gate': w_gate, 'w_up': w_up, 'w_down': w_down}


def reference(x, c, positions, ada_w, ada_b, norm1_w, w_in, q_norm_w, k_norm_w, conv_w, conv_b,
              dt_bias, a_log, d_skip, ssm_norm_w, w_out, norm2_w, router_group_w, router_group_b,
              router_expert_w, router_expert_b, w_gate, w_up, w_down):
    cos, sin = rope_tables(positions)
    c_act = jax.nn.silu(c)
    for l in range(DEPTH):
        mod = jnp.einsum('bd,de->be', c_act, ada_w[l]) + ada_b[l]
        sh1, sc1, g1, sh2, sc2, g2 = jnp.split(mod[:, None, :], 6, axis=-1)
        h = rms_norm(x, norm1_w[l]) * (1.0 + sc1) + sh1
        x = x + g1 * hybrid_mixer(h, cos, sin, w_in[l], q_norm_w[l], k_norm_w[l], conv_w[l],
                                  conv_b[l], dt_bias[l], a_log[l], d_skip[l], ssm_norm_w[l],
                                  w_out[l])
        h = rms_norm(x, norm2_w[l]) * (1.0 + sc2) + sh2
        x = x + g2 * hierarchical_moe(h, router_group_w[l], router_group_b[l], router_expert_w[l],
                                      router_expert_b[l], w_gate[l], w_up[l], w_down[l])
    return x
```

```python
import functools

import jax
import jax.numpy as jnp
from jax import lax
from jax.experimental import pallas as pl
from jax.experimental.pallas import tpu as pltpu

F32 = jnp.float32
BF16 = jnp.bfloat16

D_MODEL = 2048
D_ATTN = 1024
D_SSM = 1024
HEAD_DIM = 128
N_ATTN_HEADS = 8
ROT_DIM = 32
ROPE_THETA = 500000.0
ATTN_BLOCK = 128
DILATIONS = (1, 4, 16)
SSM_HEAD_DIM = 64
N_SSM_HEADS = 16
SSM_GROUPS = 8
SSM_STATE = 128
CONV_WIDTH = 4
SSD_CHUNK = 128
N_MAIN = 3 * D_ATTN + D_SSM + D_SSM + 2 * SSM_GROUPS * SSM_STATE
N_EXPERT_GROUPS = 8
EXPERTS_PER_GROUP = 8
N_EXPERTS = 64
D_EXPERT = 512
EPS = 1e-6

LANES = 128
SUBLANES = 8
VMEM_LIMIT = 56 * 1024 * 1024

ADALN_TN = 1024
INPROJ_TM = 1024
INPROJ_TN = 1024
OUT_TM = 512
MOE_ROWS = 256
COMBINE_TM = 256


def _cparams(sem):
    return pltpu.CompilerParams(dimension_semantics=sem, vmem_limit_bytes=VMEM_LIMIT)


def _adaln_kernel(c_ref, w_ref, b_ref, o_ref):
    c = c_ref[...]
    ca = (c * jax.nn.sigmoid(c)).astype(BF16)
    o_ref[...] = jnp.dot(ca, w_ref[...].astype(BF16), preferred_element_type=F32) + b_ref[...]


def _adaln(c, ada_w, ada_b):
    b, d = c.shape
    n = ada_w.shape[1]
    return pl.pallas_call(
        _adaln_kernel,
        grid=(n // ADALN_TN,),
        in_specs=[pl.BlockSpec((b, d), lambda j: (0, 0)),
                  pl.BlockSpec((d, ADALN_TN), lambda j: (0, j)),
                  pl.BlockSpec((1, ADALN_TN), lambda j: (0, j))],
        out_specs=pl.BlockSpec((b, ADALN_TN), lambda j: (0, j)),
        out_shape=jax.ShapeDtypeStruct((b, n), F32),
        compiler_params=_cparams(("arbitrary",)),
        name="adaln",
    )(c, ada_w, ada_b.reshape(1, n))


def _modulated_norm(x, nw, shift, scale):
    y = x * lax.rsqrt(jnp.mean(x * x, axis=-1, keepdims=True) + EPS) * nw
    return y * (1.0 + scale) + shift


def _inproj_kernel(x_ref, mod_ref, nw_ref, w_ref, wdt_ref, o_ref, dt_ref, h_scr):
    @pl.when(pl.program_id(1) == 0)
    def _():
        h = _modulated_norm(x_ref[...], nw_ref[...], mod_ref[0, 0:1, :], mod_ref[0, 1:2, :])
        h_scr[...] = h.astype(BF16)
        dt_ref[...] = jnp.dot(h_scr[...], wdt_ref[...], preferred_element_type=F32)

    o_ref[...] = jnp.dot(h_scr[...], w_ref[...], preferred_element_type=F32).astype(BF16)


def _in_proj(x2, mod3, norm_w, w_main, w_dt, seq):
    t, d = x2.shape
    tm, tn = INPROJ_TM, INPROJ_TN
    return pl.pallas_call(
        _inproj_kernel,
        grid=(t // tm, N_MAIN // tn),
        in_specs=[pl.BlockSpec((tm, d), lambda i, j: (i, 0)),
                  pl.BlockSpec((1, 6, d), lambda i, j: (i * tm // seq, 0, 0)),
                  pl.BlockSpec((1, d), lambda i, j: (0, 0)),
                  pl.BlockSpec((d, tn), lambda i, j: (0, j)),
                  pl.BlockSpec((d, LANES), lambda i, j: (0, 0))],
        out_specs=[pl.BlockSpec((tm, tn), lambda i, j: (i, j)),
                   pl.BlockSpec((tm, LANES), lambda i, j: (i, 0))],
        out_shape=[jax.ShapeDtypeStruct((t, N_MAIN), BF16),
                   jax.ShapeDtypeStruct((t, LANES), F32)],
        scratch_shapes=[pltpu.VMEM((tm, d), BF16)],
        compiler_params=_cparams(("parallel", "arbitrary")),
        name="in_proj",
    )(x2, mod3, norm_w, w_main, w_dt)


def _attn_kernel(q_ref, k_ref, v_ref, pos_ref, freq_ref, qw_ref, kw_ref, o_ref,
                 cos_s, sa_s, sb_s, qs, ks, vs, acc_s, m_s, l_s):
    seq = q_ref.shape[0]
    blk = ATTN_BLOCK
    half = ROT_DIM // 2

    @pl.when(pl.program_id(1) == 0)
    def _():
        ang = pos_ref[0].astype(F32) * freq_ref[...]
        lane = lax.broadcasted_iota(jnp.int32, (seq, LANES), 1)
        sn = jnp.sin(ang)
        cos_s[...] = jnp.cos(ang)
        sa_s[...] = jnp.where(lane < half, -sn, 0.0)
        sb_s[...] = jnp.where((lane >= half) & (lane < ROT_DIM), sn, 0.0)

    def norm_rope(t_ref, w_ref):
        t = t_ref[...].astype(F32)
        y = t * lax.rsqrt(jnp.mean(t * t, axis=-1, keepdims=True) + EPS) * w_ref[...]
        return (y * cos_s[...] + pltpu.roll(y, LANES - half, 1) * sa_s[...]
                + pltpu.roll(y, half, 1) * sb_s[...])

    qs[...] = norm_rope(q_ref, qw_ref)
    ks[...] = norm_rope(k_ref, kw_ref)
    vs[...] = v_ref[...].astype(F32)

    qi = lax.broadcasted_iota(jnp.int32, (blk, blk), 0)
    kj = lax.broadcasted_iota(jnp.int32, (blk, blk), 1)
    mask_cur = kj <= qi
    mask_prev = kj >= qi
    scale = HEAD_DIM ** -0.5
    neg_inf = -jnp.inf

    def rows(start, d):
        return pl.ds(start, blk) if d == 1 else pl.ds(start, blk, stride=d)

    def block(start, d, with_prev, first_pattern):
        r_cur = rows(start, d)
        qb = qs[r_cur, :].astype(BF16)
        kb = ks[r_cur, :].astype(BF16)
        vb = vs[r_cur, :].astype(BF16)
        nt = (((1,), (1,)), ((), ()))
        s_cur = lax.dot_general(qb, kb, nt, preferred_element_type=F32) * scale
        s_cur = jnp.where(mask_cur, s_cur, neg_inf)
        m = jnp.max(s_cur, axis=-1, keepdims=True)
        if with_prev:
            r_prev = rows(start - blk * d, d)
            kp = ks[r_prev, :].astype(BF16)
            vp = vs[r_prev, :].astype(BF16)
            s_prev = lax.dot_general(qb, kp, nt, preferred_element_type=F32) * scale
            s_prev = jnp.where(mask_prev, s_prev, neg_inf)
            m = jnp.maximum(m, jnp.max(s_prev, axis=-1, keepdims=True))
        p_cur = jnp.exp(s_cur - m)
        l = jnp.sum(p_cur, axis=-1, keepdims=True)
        acc = jnp.dot(p_cur.astype(BF16), vb, preferred_element_type=F32)
        if with_prev:
            p_prev = jnp.exp(s_prev - m)
            l = l + jnp.sum(p_prev, axis=-1, keepdims=True)
            acc = acc + jnp.dot(p_prev.astype(BF16), vp, preferred_element_type=F32)
        mb = jnp.broadcast_to(m, (blk, LANES))
        lb = jnp.broadcast_to(l, (blk, LANES))
        if first_pattern:
            m_s[r_cur, :] = mb
            l_s[r_cur, :] = lb
            acc_s[r_cur, :] = acc
        else:
            m_old = m_s[r_cur, :]
            m_new = jnp.maximum(m_old, mb)
            a_old = jnp.exp(m_old - m_new)
            a_new = jnp.exp(mb - m_new)
            m_s[r_cur, :] = m_new
            l_s[r_cur, :] = a_old * l_s[r_cur, :] + a_new * lb
            acc_s[r_cur, :] = a_old * acc_s[r_cur, :] + a_new * acc

    for d in DILATIONS:
        sub_len = seq // d
        n_blk = sub_len // blk
        first_pattern = d == DILATIONS[0]
        if n_blk == 1:
            def body_r(r, carry, d=d, first_pattern=first_pattern):
                block(r, d, False, first_pattern)
                return carry
            lax.fori_loop(0, d, body_r, 0)
        else:
            for r in range(d):
                block(r, d, False, first_pattern)

                def body_n(n, carry, r=r, d=d, first_pattern=first_pattern):
                    block(r + n * (blk * d), d, True, first_pattern)
                    return carry
                lax.fori_loop(1, n_blk, body_n, 0)

    o_ref[...] = (acc_s[...] / l_s[...]).astype(BF16)


def _attention(proj, pos3, freq, qw, kw, batch, seq):
    t = proj.shape[0]
    hd = HEAD_DIM
    nh = N_ATTN_HEADS
    scr = pltpu.VMEM((seq, hd), F32)
    return pl.pallas_call(
        _attn_kernel,
        grid=(batch, nh),
        in_specs=[pl.BlockSpec((seq, hd), lambda b, h: (b, h)),
                  pl.BlockSpec((seq, hd), lambda b, h: (b, nh + h)),
                  pl.BlockSpec((seq, hd), lambda b, h: (b, 2 * nh + h)),
                  pl.BlockSpec((1, seq, 1), lambda b, h: (b, 0, 0)),
                  pl.BlockSpec((1, hd), lambda b, h: (0, 0)),
                  pl.BlockSpec((1, hd), lambda b, h: (0, 0)),
                  pl.BlockSpec((1, hd), lambda b, h: (0, 0))],
        out_specs=pl.BlockSpec((seq, hd), lambda b, h: (b, h)),
        out_shape=jax.ShapeDtypeStruct((t, D_ATTN), BF16),
        scratch_shapes=[scr] * 9,
        compiler_params=_cparams(("parallel", "arbitrary")),
        name="attention",
    )(proj, proj, proj, pos3, freq, qw, kw)


def _split3(v):
    hi = v.astype(BF16)
    r1 = v - hi.astype(F32)
    mid = r1.astype(BF16)
    lo = (r1 - mid.astype(F32)).astype(BF16)
    return hi, mid, lo


def _ssd_kernel(xs_ref, bm_ref, cm_ref, z_ref, dtr_ref, cwx_ref, cwb_ref, cwc_ref,
                cbx_ref, cbb_ref, cbc_ref, dtb_ref, alog_ref, dsk_ref, nw_ref, o_ref,
                xpad, x_s, b_s, c_s, dt_s, da_s):
    seq = xs_ref.shape[0]
    ck = SSD_CHUNK
    g = pl.program_id(1)
    pad = SUBLANES

    def conv_silu(src_ref, w_ref, b_ref, dst):
        xpad[0:pad, :] = jnp.zeros((pad, LANES), F32)
        xpad[pad:, :] = src_ref[...].astype(F32)
        y = b_ref[...] + jnp.zeros((seq, LANES), F32)
        for j in range(CONV_WIDTH):
            y = y + w_ref[j:j + 1, :] * xpad[pl.ds(pad - (CONV_WIDTH - 1) + j, seq), :]
        dst[...] = y * jax.nn.sigmoid(y)

    conv_silu(xs_ref, cwx_ref, cbx_ref, x_s)
    conv_silu(bm_ref, cwb_ref, cbb_ref, b_s)
    conv_silu(cm_ref, cwc_ref, cbc_ref, c_s)

    xr = dtr_ref[...] + dtb_ref[...]
    dt = jnp.maximum(xr, 0.0) + jnp.log1p(jnp.exp(-jnp.abs(xr)))
    dt_s[...] = dt
    da_s[...] = dt * (-jnp.exp(alog_ref[...]))

    li = lax.broadcasted_iota(jnp.int32, (ck, ck), 0)
    si = lax.broadcasted_iota(jnp.int32, (ck, ck), 1)
    causal = si <= li
    eye = si == li
    tril_bf = causal.astype(BF16)
    lane = lax.broadcasted_iota(jnp.int32, (ck, LANES), 1)
    lane_row = lax.broadcasted_iota(jnp.int32, (1, LANES), 1)
    head1 = lane >= SSM_HEAD_DIM

    def pick(v, idx):
        sel = lane if v.shape[0] == ck else lane_row
        return jnp.sum(jnp.where(sel == idx, v, 0.0), axis=-1, keepdims=True)

    def chunk(c, prev):
        r = pl.ds(pl.multiple_of(c * ck, ck), ck)
        xc = x_s[r, :]
        bc = b_s[r, :].astype(BF16)
        cc = c_s[r, :].astype(BF16)
        dtc = dt_s[r, :]
        hi, mid, lo = _split3(da_s[r, :])
        acs = (jnp.dot(tril_bf, hi, preferred_element_type=F32)
               + jnp.dot(tril_bf, mid, preferred_element_type=F32)
               + jnp.dot(tril_bf, lo, preferred_element_type=F32))
        acs_last = acs[ck - 1:ck, :]
        cb = lax.dot_general(cc, bc, (((1,), (1,)), ((), ())), preferred_element_type=F32)
        y_diag = []
        dt_cols, acs_cols, last = [], [], []
        for k in range(2):
            hidx = 2 * g + k
            dt_col = pick(dtc, hidx)
            acs_col = pick(acs, hidx)
            acs_row = jnp.sum(jnp.where(eye, acs_col, 0.0), axis=0, keepdims=True)
            decay = jnp.exp(jnp.where(causal, acs_col - acs_row, -jnp.inf))
            xk = xc[:, k * SSM_HEAD_DIM:(k + 1) * SSM_HEAD_DIM] * dt_col
            y_diag.append(jnp.dot((cb * decay).astype(BF16), xk.astype(BF16),
                                  preferred_element_type=F32))
            dt_cols.append(dt_col)
            acs_cols.append(acs_col)
            last.append(pick(acs_last, hidx))
        dt_b = jnp.where(head1, dt_cols[1], dt_cols[0])
        acs_b = jnp.where(head1, acs_cols[1], acs_cols[0])
        last_b = jnp.where(head1, last[1], last[0])
        xdt = xc * dt_b
        y_off = jnp.dot(cc, prev.astype(BF16), preferred_element_type=F32) * jnp.exp(acs_b)
        y = jnp.concatenate(y_diag, axis=-1) + y_off + dsk_ref[...] * xc
        zc = z_ref[r, :].astype(F32)
        y = y * (zc * jax.nn.sigmoid(zc))
        y = y * lax.rsqrt(jnp.mean(y * y, axis=-1, keepdims=True) + EPS) * nw_ref[...]
        o_ref[r, :] = y.astype(BF16)
        w = (xdt * jnp.exp(last_b - acs_b)).astype(BF16)
        st = lax.dot_general(bc, w, (((0,), (0,)), ((), ())), preferred_element_type=F32)
        last_row = jnp.where(lane_row >= SSM_HEAD_DIM, last[1], last[0])
        return prev * jnp.exp(last_row) + st

    lax.fori_loop(0, seq // ck, chunk, jnp.zeros((SSM_STATE, LANES), F32))


def _ssd(proj, dt_raw, conv_w, conv_b, dtb, alog, dsk, nw, batch, seq):
    t = proj.shape[0]
    gs = SSM_GROUPS
    col = lambda off: (lambda b, g: (b, off + g))
    wcol = lambda off: (lambda b, g: (0, off + g))
    zq, xq, bq, cq = 3 * D_ATTN // LANES, 4 * D_ATTN // LANES, 5 * D_ATTN // LANES, 6 * D_ATTN // LANES
    const = lambda b, g: (0, 0)
    scr = pltpu.VMEM((seq, LANES), F32)
    return pl.pallas_call(
        _ssd_kernel,
        grid=(batch, gs),
        in_specs=[pl.BlockSpec((seq, LANES), col(xq)),
                  pl.BlockSpec((seq, LANES), col(bq)),
                  pl.BlockSpec((seq, LANES), col(cq)),
                  pl.BlockSpec((seq, LANES), col(zq)),
                  pl.BlockSpec((seq, LANES), lambda b, g: (b, 0)),
                  pl.BlockSpec((CONV_WIDTH, LANES), wcol(0)),
                  pl.BlockSpec((CONV_WIDTH, LANES), wcol(gs)),
                  pl.BlockSpec((CONV_WIDTH, LANES), wcol(2 * gs)),
                  pl.BlockSpec((1, LANES), wcol(0)),
                  pl.BlockSpec((1, LANES), wcol(gs)),
                  pl.BlockSpec((1, LANES), wcol(2 * gs)),
                  pl.BlockSpec((1, LANES), const),
                  pl.BlockSpec((1, LANES), const),
                  pl.BlockSpec((1, LANES), lambda b, g: (0, g)),
                  pl.BlockSpec((1, LANES), lambda b, g: (0, g))],
        out_specs=pl.BlockSpec((seq, LANES), lambda b, g: (b, g)),
        out_shape=jax.ShapeDtypeStruct((t, D_SSM), BF16),
        scratch_shapes=[pltpu.VMEM((seq + SUBLANES, LANES), F32), scr, scr, scr, scr, scr],
        compiler_params=_cparams(("parallel", "arbitrary")),
        name="ssd",
    )(proj, proj, proj, proj, dt_raw, conv_w, conv_w, conv_w, conv_b, conv_b, conv_b,
      dtb, alog, dsk, nw)


def _out_router_kernel(ya_ref, ys_ref, wa_ref, ws_ref, x_ref, mod_ref, nw_ref, rw_ref, rb_ref,
                       x1_ref, h2_ref, eid_ref, ew_ref):
    tm = x_ref.shape[0]
    mix = (jnp.dot(ya_ref[...], wa_ref[...], preferred_element_type=F32)
           + jnp.dot(ys_ref[...], ws_ref[...], preferred_element_type=F32))
    x1 = x_ref[...] + mod_ref[0, 2:3, :] * mix
    x1_ref[...] = x1
    h2 = _modulated_norm(x1, nw_ref[...], mod_ref[0, 3:4, :], mod_ref[0, 4:5, :])
    h2_ref[...] = h2

    h_hi = h2.astype(BF16)
    h_lo = (h2 - h_hi.astype(F32)).astype(BF16)
    rw = rw_ref[...]
    w_hi = rw.astype(BF16)
    w_lo = (rw - w_hi.astype(F32)).astype(BF16)
    nt = (((1,), (1,)), ((), ()))
    logits = (lax.dot_general(w_hi, h_hi, nt, preferred_element_type=F32)
              + lax.dot_general(w_hi, h_lo, nt, preferred_element_type=F32)
              + lax.dot_general(w_lo, h_hi, nt, preferred_element_type=F32)) + rb_ref[...]

    ng, ne = N_EXPERT_GROUPS, EXPERTS_PER_GROUP
    iota = lax.broadcasted_iota(jnp.int32, (ne, tm), 0)

    def argmax0(v):
        mx = jnp.max(v, axis=0, keepdims=True)
        idx = jnp.min(jnp.where(v == mx, iota, ne), axis=0, keepdims=True)
        return mx, idx

    gl = logits[N_EXPERTS:N_EXPERTS + ng, :]
    gmax, gidx = argmax0(gl)
    g_gate = 1.0 / jnp.sum(jnp.exp(gl - gmax), axis=0, keepdims=True)
    e_in = jnp.zeros((ne, tm), F32)
    for gi in range(ng):
        e_in = e_in + jnp.where(gidx == gi, logits[gi * ne:(gi + 1) * ne, :], 0.0)
    m1, i1 = argmax0(e_in)
    m2, i2 = argmax0(jnp.where(iota == i1, -jnp.inf, e_in))
    p2 = jnp.exp(m2 - m1)
    den = 1.0 + p2
    eid_ref[0:1, :] = gidx * ne + i1
    eid_ref[1:2, :] = gidx * ne + i2
    ew_ref[0:1, :] = (1.0 / den) * g_gate
    ew_ref[1:2, :] = (p2 / den) * g_gate


def _out_router(ya, ys, wa, ws, x2, mod3, norm_w, rw, rb, seq):
    t, d = x2.shape
    tm = OUT_TM
    const = lambda i: (0, 0)
    return pl.pallas_call(
        _out_router_kernel,
        grid=(t // tm,),
        in_specs=[pl.BlockSpec((tm, D_ATTN), lambda i: (i, 0)),
                  pl.BlockSpec((tm, D_SSM), lambda i: (i, 0)),
                  pl.BlockSpec((D_ATTN, d), const, pipeline_mode=pl.Buffered(1)),
                  pl.BlockSpec((D_SSM, d), const, pipeline_mode=pl.Buffered(1)),
                  pl.BlockSpec((tm, d), lambda i: (i, 0)),
                  pl.BlockSpec((1, 6, d), lambda i: (i * tm // seq, 0, 0)),
                  pl.BlockSpec((1, d), const),
                  pl.BlockSpec((LANES, d), const, pipeline_mode=pl.Buffered(1)),
                  pl.BlockSpec((LANES, 1), const)],
        out_specs=[pl.BlockSpec((tm, d), lambda i: (i, 0)),
                   pl.BlockSpec((tm, d), lambda i: (i, 0)),
                   pl.BlockSpec((2, tm), lambda i: (0, i)),
                   pl.BlockSpec((2, tm), lambda i: (0, i))],
        out_shape=[jax.ShapeDtypeStruct((t, d), F32),
                   jax.ShapeDtypeStruct((t, d), F32),
                   jax.ShapeDtypeStruct((2, t), jnp.int32),
                   jax.ShapeDtypeStruct((2, t), F32)],
        compiler_params=_cparams(("parallel",)),
        name="out_router",
    )(ya, ys, wa, ws, x2, mod3, norm_w, rw, rb)


def _gather_rows(src_hbm, idx_ref, dst, sem, n_rows, dst_off=0):
    def issue(r, carry):
        pltpu.make_async_copy(src_hbm.at[pl.ds(idx_ref[r], 1)],
                              dst.at[pl.ds(dst_off + r, 1)], sem).start()
        return carry
    lax.fori_loop(0, n_rows, issue, 0, unroll=8)


def _wait_rows(src_hbm, dst, sem, n_rows, dst_off=0):
    pltpu.make_async_copy(src_hbm.at[pl.ds(0, n_rows)], dst.at[pl.ds(dst_off, n_rows)], sem).wait()


def _experts_kernel(be_ref, nused_ref, tok_ref, tok_next_ref, h_hbm, wg_ref, wu_ref, wd_ref, o_ref,
                    xbuf, sems, wg_s, wu_s, wd_s):
    i = pl.program_id(0)
    n_used = nused_ref[0]
    rows = o_ref.shape[0]
    slot = i % 2

    @pl.when(i == 0)
    def _():
        _gather_rows(h_hbm, tok_ref.at[0, 0], xbuf.at[0], sems.at[0], rows)

    @pl.when(i + 1 < n_used)
    def _():
        _gather_rows(h_hbm, tok_next_ref.at[0, 0], xbuf.at[1 - slot], sems.at[1 - slot], rows)

    changed = jnp.logical_or(i == 0, be_ref[i] != be_ref[jnp.maximum(i - 1, 0)])

    @pl.when(jnp.logical_and(changed, i < n_used))
    def _():
        wg_s[...] = wg_ref[0].astype(BF16)
        wu_s[...] = wu_ref[0].astype(BF16)
        wd_s[...] = wd_ref[0].astype(BF16)

    @pl.when(i < n_used)
    def _():
        _wait_rows(h_hbm, xbuf.at[slot], sems.at[slot], rows)
        xb = xbuf[slot].astype(BF16)
        gt = jnp.dot(xb, wg_s[...], preferred_element_type=F32)
        up = jnp.dot(xb, wu_s[...], preferred_element_type=F32)
        hid = (gt * jax.nn.sigmoid(gt)) * up
        o_ref[...] = jnp.dot(hid.astype(BF16), wd_s[...], preferred_element_type=F32)

    @pl.when(i >= n_used)
    def _():
        o_ref[...] = jnp.zeros(o_ref.shape, F32)


def _experts(block_e, n_used, row_tok3, h2, w_gate, w_up, w_down):
    n_blocks = row_tok3.shape[0]
    rows = MOE_ROWS
    d = h2.shape[1]
    de = w_gate.shape[2]
    last = n_blocks - 1
    grid_spec = pltpu.PrefetchScalarGridSpec(
        num_scalar_prefetch=2,
        grid=(n_blocks,),
        in_specs=[pl.BlockSpec((1, 1, rows), lambda i, be, nu: (i, 0, 0), memory_space=pltpu.SMEM),
                  pl.BlockSpec((1, 1, rows), lambda i, be, nu: (jnp.minimum(i + 1, last), 0, 0),
                               memory_space=pltpu.SMEM),
                  pl.BlockSpec(memory_space=pl.ANY),
                  pl.BlockSpec((1, d, de), lambda i, be, nu: (be[i], 0, 0)),
                  pl.BlockSpec((1, d, de), lambda i, be, nu: (be[i], 0, 0)),
                  pl.BlockSpec((1, de, d), lambda i, be, nu: (be[i], 0, 0))],
        out_specs=pl.BlockSpec((rows, d), lambda i, be, nu: (i, 0)),
        scratch_shapes=[pltpu.VMEM((2, rows, d), F32),
                        pltpu.SemaphoreType.DMA((2,)),
                        pltpu.VMEM((d, de), BF16),
                        pltpu.VMEM((d, de), BF16),
                        pltpu.VMEM((de, d), BF16)],
    )
    return pl.pallas_call(
        _experts_kernel,
        grid_spec=grid_spec,
        out_shape=jax.ShapeDtypeStruct((n_blocks * rows, d), F32),
        compiler_params=_cparams(("arbitrary",)),
        name="experts",
    )(block_e, n_used, row_tok3, row_tok3, h2, w_gate, w_up, w_down)


def _combine_kernel(d0_ref, d1_ref, d0n_ref, d1n_ref, yd_hbm, x1_ref, mod_ref, ew_ref, o_ref,
                    ybuf, sems):
    i = pl.program_id(0)
    n = pl.num_programs(0)
    tm = x1_ref.shape[0]
    slot = i % 2

    def gather(a_ref, b_ref, s):
        _gather_rows(yd_hbm, a_ref.at[0, 0], ybuf.at[s], sems.at[s], tm, 0)
        _gather_rows(yd_hbm, b_ref.at[0, 0], ybuf.at[s], sems.at[s], tm, tm)

    @pl.when(i == 0)
    def _():
        gather(d0_ref, d1_ref, 0)

    @pl.when(i + 1 < n)
    def _():
        gather(d0n_ref, d1n_ref, 1 - slot)

    _wait_rows(yd_hbm, ybuf.at[slot], sems.at[slot], 2 * tm)
    y0 = ybuf[slot, 0:tm, :]
    y1 = ybuf[slot, tm:2 * tm, :]
    moe = ew_ref[:, 0:1] * y0 + ew_ref[:, 1:2] * y1
    o_ref[...] = x1_ref[...] + mod_ref[0, 5:6, :] * moe


def _combine(dest0, dest1, yd, x1, mod3, ew_t, seq):
    t, d = x1.shape
    tm = COMBINE_TM
    n = t // tm
    cur = lambda i: (i, 0, 0)
    nxt = lambda i: (jnp.minimum(i + 1, n - 1), 0, 0)
    smem = functools.partial(pl.BlockSpec, (1, 1, tm), memory_space=pltpu.SMEM)
    return pl.pallas_call(
        _combine_kernel,
        grid=(n,),
        in_specs=[smem(cur), smem(cur), smem(nxt), smem(nxt),
                  pl.BlockSpec(memory_space=pl.ANY),
                  pl.BlockSpec((tm, d), lambda i: (i, 0)),
                  pl.BlockSpec((1, 6, d), lambda i: (i * tm // seq, 0, 0)),
                  pl.BlockSpec((tm, 2), lambda i: (i, 0))],
        out_specs=pl.BlockSpec((tm, d), lambda i: (i, 0)),
        out_shape=jax.ShapeDtypeStruct((t, d), F32),
        scratch_shapes=[pltpu.VMEM((2, 2 * tm, d), F32), pltpu.SemaphoreType.DMA((2,))],
        compiler_params=_cparams(("arbitrary",)),
        name="combine",
    )(dest0, dest1, dest0, dest1, yd, x1, mod3, ew_t)


def _dispatch(eid, t):
    rows = MOE_ROWS
    n_assign = 2 * t
    n_blocks = (n_assign + N_EXPERTS * (rows - 1)) // rows + 1
    flat_e = eid.T.reshape(-1)
    order = jnp.argsort(flat_e).astype(jnp.int32)
    se = flat_e[order]
    stok = order // 2
    counts = jnp.zeros((N_EXPERTS,), jnp.int32).at[flat_e].add(1)
    start = jnp.cumsum(counts) - counts
    padded = (counts + rows - 1) // rows * rows
    pend = jnp.cumsum(padded)
    pstart = pend - padded
    dest_sorted = pstart[se] + (jnp.arange(n_assign, dtype=jnp.int32) - start[se])
    row_tok = jnp.zeros((n_blocks * rows,), jnp.int32).at[dest_sorted].set(stok)
    dest = jnp.zeros((n_assign,), jnp.int32).at[order].set(dest_sorted)
    block_start = jnp.arange(n_blocks, dtype=jnp.int32) * rows
    block_e = jnp.minimum(jnp.searchsorted(pend, block_start, side='right'),
                          N_EXPERTS - 1).astype(jnp.int32)
    n_used = (pend[-1] // rows).astype(jnp.int32).reshape(1)
    return block_e, n_used, row_tok.reshape(n_blocks, 1, rows), dest.reshape(t, 2)


def kernel(x, c, positions, ada_w, ada_b, norm1_w, w_in, q_norm_w, k_norm_w, conv_w, conv_b,
           dt_bias, a_log, d_skip, ssm_norm_w, w_out, norm2_w, router_group_w, router_group_b,
           router_expert_w, router_expert_b, w_gate, w_up, w_down):
    batch, seq, d = x.shape
    t = batch * seq
    depth = ada_w.shape[0]
    nh_pad = LANES - N_SSM_HEADS
    inv_freq = ROPE_THETA ** (-(jnp.arange(0, ROT_DIM, 2, dtype=F32) / ROT_DIM))
    freq = jnp.concatenate([inv_freq, inv_freq, jnp.zeros((HEAD_DIM - ROT_DIM,), F32)]).reshape(1, HEAD_DIM)
    pos3 = positions.reshape(batch, seq, 1)
    xc = x.reshape(t, d)
    for l in range(depth):
        mod3 = _adaln(c, ada_w[l], ada_b[l]).reshape(batch, 6, d)
        w_main = w_in[l][:, :N_MAIN].astype(BF16)
        w_dt = jnp.pad(w_in[l][:, N_MAIN:], ((0, 0), (0, nh_pad))).astype(BF16)
        proj, dt_raw = _in_proj(xc, mod3, norm1_w[l].reshape(1, d), w_main, w_dt, seq)
        y_attn = _attention(proj, pos3, freq, q_norm_w[l].reshape(1, HEAD_DIM),
                            k_norm_w[l].reshape(1, HEAD_DIM), batch, seq)
        y_ssm = _ssd(proj, dt_raw, conv_w[l], conv_b[l].reshape(1, -1),
                     jnp.pad(dt_bias[l], (0, nh_pad)).reshape(1, LANES),
                     jnp.pad(a_log[l], (0, nh_pad)).reshape(1, LANES),
                     jnp.repeat(d_skip[l], SSM_HEAD_DIM).reshape(1, D_SSM),
                     ssm_norm_w[l].reshape(1, D_SSM), batch, seq)
        rw = jnp.concatenate([router_expert_w[l].T, router_group_w[l].T,
                              jnp.zeros((LANES - N_EXPERTS - N_EXPERT_GROUPS, d), F32)])
        rb = jnp.concatenate([router_expert_b[l], router_group_b[l],
                              jnp.zeros((LANES - N_EXPERTS - N_EXPERT_GROUPS,), F32)]).reshape(LANES, 1)
        wo = w_out[l].astype(BF16)
        x1, h2, eid, ew = _out_router(y_attn, y_ssm, wo[:D_ATTN], wo[D_ATTN:], xc, mod3,
                                      norm2_w[l].reshape(1, d), rw, rb, seq)
        block_e, n_used, row_tok3, dest = _dispatch(eid, t)
        yd = _experts(block_e, n_used, row_tok3, h2, w_gate[l], w_up[l], w_down[l])
        n_ct = t // COMBINE_TM
        xc = _combine(dest[:, 0].reshape(n_ct, 1, COMBINE_TM), dest[:, 1].reshape(n_ct, 1, COMBINE_TM),
                      yd, x1, mod3, ew.T, seq)
    return xc.reshape(batch, seq, d)
```

```python
import jax
import jax.numpy as jnp
from jax import lax
from jax.experimental import pallas as pl
from jax.experimental.pallas import tpu as pltpu

F32 = jnp.float32
BF16 = jnp.bfloat16

D_MODEL = 2048
D_ATTN = 1024
D_SSM = 1024
HEAD_DIM = 128
N_ATTN_HEADS = 8
ROT_DIM = 32
ROPE_THETA = 500000.0
ATTN_BLOCK = 128
DILATIONS = (1, 4, 16)
SSM_HEAD_DIM = 64
N_SSM_HEADS = 16
SSM_GROUPS = 8
SSM_STATE = 128
CONV_WIDTH = 4
SSD_CHUNK = 128
N_MAIN = 3 * D_ATTN + D_SSM + D_SSM + 2 * SSM_GROUPS * SSM_STATE
N_EXPERT_GROUPS = 8
EXPERTS_PER_GROUP = 8
N_EXPERTS = 64
D_EXPERT = 512
EPS = 1e-6

LANES = 128
SUBLANES = 8
VMEM_LIMIT = 56 * 1024 * 1024

ADALN_TN = 1024
INPROJ_TM = 1024
INPROJ_TN = 1024
ATTN_GROUP = 5
OUT_TM = 512
MOE_ROWS = 256


def _cparams(sem):
    return pltpu.CompilerParams(dimension_semantics=sem, vmem_limit_bytes=VMEM_LIMIT)


def _adaln_kernel(c_ref, w_ref, b_ref, o_ref):
    c = c_ref[...]
    ca = (c * jax.nn.sigmoid(c)).astype(BF16)
    o_ref[...] = jnp.dot(ca, w_ref[...].astype(BF16), preferred_element_type=F32) + b_ref[...]


def _adaln(c, ada_w, ada_b):
    b, d = c.shape
    n = ada_w.shape[1]
    return pl.pallas_call(
        _adaln_kernel,
        grid=(n // ADALN_TN,),
        in_specs=[pl.BlockSpec((b, d), lambda j: (0, 0)),
                  pl.BlockSpec((d, ADALN_TN), lambda j: (0, j)),
                  pl.BlockSpec((1, ADALN_TN), lambda j: (0, j))],
        out_specs=pl.BlockSpec((b, ADALN_TN), lambda j: (0, j)),
        out_shape=jax.ShapeDtypeStruct((b, n), F32),
        compiler_params=_cparams(("arbitrary",)),
        name="adaln",
    )(c, ada_w, ada_b.reshape(1, n))


def _modulated_norm(x, nw, shift, scale):
    y = x * lax.rsqrt(jnp.mean(x * x, axis=-1, keepdims=True) + EPS) * nw
    return y * (1.0 + scale) + shift


def _inproj_kernel(x_ref, mod_ref, nw_ref, w_ref, wdt_ref, o_ref, dt_ref, h_scr):
    @pl.when(pl.program_id(1) == 0)
    def _():
        h = _modulated_norm(x_ref[...], nw_ref[...], mod_ref[0, 0:1, :], mod_ref[0, 1:2, :])
        h_scr[...] = h.astype(BF16)
        dt_ref[...] = jnp.dot(h_scr[...], wdt_ref[...], preferred_element_type=F32)

    o_ref[...] = jnp.dot(h_scr[...], w_ref[...], preferred_element_type=F32).astype(BF16)


def _in_proj(x2, mod3, norm_w, w_main, w_dt, seq):
    t, d = x2.shape
    tm, tn = INPROJ_TM, INPROJ_TN
    return pl.pallas_call(
        _inproj_kernel,
        grid=(t // tm, N_MAIN // tn),
        in_specs=[pl.BlockSpec((tm, d), lambda i, j: (i, 0)),
                  pl.BlockSpec((1, 6, d), lambda i, j: (i * tm // seq, 0, 0)),
                  pl.BlockSpec((1, d), lambda i, j: (0, 0)),
                  pl.BlockSpec((d, tn), lambda i, j: (0, j)),
                  pl.BlockSpec((d, LANES), lambda i, j: (0, 0))],
        out_specs=[pl.BlockSpec((tm, tn), lambda i, j: (i, j)),
                   pl.BlockSpec((tm, LANES), lambda i, j: (i, 0))],
        out_shape=[jax.ShapeDtypeStruct((t, N_MAIN), BF16),
                   jax.ShapeDtypeStruct((t, LANES), F32)],
        scratch_shapes=[pltpu.VMEM((tm, d), BF16)],
        compiler_params=_cparams(("parallel", "arbitrary")),
        name="in_proj",
    )(x2, mod3, norm_w, w_main, w_dt)


def _attn_kernel(q_ref, k_ref, v_ref, pos_ref, freq_ref, qw_ref, kw_ref, o_ref,
                 cos_s, sa_s, sb_s, qs, ks, vs, acc_s, m_s, l_s):
    seq = q_ref.shape[0]
    blk = ATTN_BLOCK
    half = ROT_DIM // 2

    @pl.when(pl.program_id(1) == 0)
    def _():
        ang = pos_ref[0].astype(F32) * freq_ref[...]
        lane = lax.broadcasted_iota(jnp.int32, (seq, LANES), 1)
        sn = jnp.sin(ang)
        cos_s[...] = jnp.cos(ang)
        sa_s[...] = jnp.where(lane < half, -sn, 0.0)
        sb_s[...] = jnp.where((lane >= half) & (lane < ROT_DIM), sn, 0.0)

    def norm_rope(t_ref, w_ref):
        t = t_ref[...].astype(F32)
        y = t * lax.rsqrt(jnp.mean(t * t, axis=-1, keepdims=True) + EPS) * w_ref[...]
        return (y * cos_s[...] + pltpu.roll(y, LANES - half, 1) * sa_s[...]
                + pltpu.roll(y, half, 1) * sb_s[...])

    qs[...] = norm_rope(q_ref, qw_ref)
    ks[...] = norm_rope(k_ref, kw_ref)
    vs[...] = v_ref[...].astype(F32)

    qi = lax.broadcasted_iota(jnp.int32, (blk, blk), 0)
    kj = lax.broadcasted_iota(jnp.int32, (blk, blk), 1)
    mask_cur = kj <= qi
    mask_prev = kj >= qi
    mask_both = jnp.concatenate([mask_prev, mask_cur], axis=1)
    ones_blk = jnp.ones((blk, LANES), BF16)
    scale = HEAD_DIM ** -0.5
    nt = (((1,), (1,)), ((), ()))

    def rows(start, d):
        return pl.ds(start, blk) if d == 1 else pl.ds(start, blk, stride=d)

    def group(starts, d, with_prev, first_pattern):
        r_cur = [rows(s, d) for s in starts]
        qb = [qs[r, :].astype(BF16) for r in r_cur]
        kb = [ks[r, :].astype(BF16) for r in r_cur]
        vb = [jnp.concatenate([vs[r, :].astype(BF16), ones_blk], axis=1) for r in r_cur]
        if with_prev:
            r_prev = [rows(s - blk * d, d) for s in starts]
            kb = [jnp.concatenate([ks[rp, :].astype(BF16), k], axis=0) for rp, k in zip(r_prev, kb)]
            vb = [jnp.concatenate([jnp.concatenate([vs[rp, :].astype(BF16), ones_blk], axis=1), v], axis=0)
                  for rp, v in zip(r_prev, vb)]
        mask = mask_both if with_prev else mask_cur
        sc = [jnp.where(mask, lax.dot_general(q, k, nt, preferred_element_type=F32) * scale, -jnp.inf)
              for q, k in zip(qb, kb)]
        if with_prev:
            mx = [jnp.max(jnp.maximum(x[:, :blk], x[:, blk:]), axis=-1, keepdims=True) for x in sc]
        else:
            mx = [jnp.max(x, axis=-1, keepdims=True) for x in sc]
        pr = [jnp.exp(x - m).astype(BF16) for x, m in zip(sc, mx)]
        ov = [jnp.dot(p, v, preferred_element_type=F32) for p, v in zip(pr, vb)]
        mb = [jnp.broadcast_to(m, (blk, LANES)) for m in mx]
        if first_pattern:
            for r, m, o in zip(r_cur, mb, ov):
                m_s[r, :] = m
                l_s[r, :] = o[:, LANES:]
                acc_s[r, :] = o[:, :LANES]
        else:
            old = [(m_s[r, :], l_s[r, :], acc_s[r, :]) for r in r_cur]
            new = []
            for (m_old, l_old, acc_old), m, o in zip(old, mb, ov):
                m_new = jnp.maximum(m_old, m)
                a_old = jnp.exp(m_old - m_new)
                a_new = jnp.exp(m - m_new)
                new.append((m_new, a_old * l_old + a_new * o[:, LANES:],
                            a_old * acc_old + a_new * o[:, :LANES]))
            for r, (m_new, l_new, acc_new) in zip(r_cur, new):
                m_s[r, :] = m_new
                l_s[r, :] = l_new
                acc_s[r, :] = acc_new

    def loop(n_iter, starts_of, d, with_prev, first_pattern):
        def body(i, carry):
            group(starts_of(i), d, with_prev, first_pattern)
            return carry
        lax.fori_loop(0, n_iter, body, 0)

    for d in DILATIONS:
        n_blk = seq // (d * blk)
        first_pattern = d == DILATIONS[0]
        span = blk * d
        if d == 1:
            group([0], d, False, first_pattern)
        else:
            loop(d // 4, lambda i: [4 * i + j for j in range(4)], d, False, first_pattern)
        if d == 1:
            loop((n_blk - 1) // ATTN_GROUP,
                 lambda i: [(1 + ATTN_GROUP * i + j) * span for j in range(ATTN_GROUP)],
                 d, True, first_pattern)
        elif n_blk > 1:
            loop(d // 2, lambda i: [2 * i + j + n * span for j in range(2) for n in range(1, n_blk)],
                 d, True, first_pattern)

    o_ref[...] = (acc_s[...] / l_s[...]).astype(BF16)


def _attention(proj, pos3, freq, qw, kw, batch, seq):
    t = proj.shape[0]
    hd = HEAD_DIM
    nh = N_ATTN_HEADS
    assert seq % (ATTN_BLOCK * max(DILATIONS)) == 0 and (seq // ATTN_BLOCK - 1) % ATTN_GROUP == 0
    scr = pltpu.VMEM((seq, hd), F32)
    return pl.pallas_call(
        _attn_kernel,
        grid=(batch, nh),
        in_specs=[pl.BlockSpec((seq, hd), lambda b, h: (b, h)),
                  pl.BlockSpec((seq, hd), lambda b, h: (b, nh + h)),
                  pl.BlockSpec((seq, hd), lambda b, h: (b, 2 * nh + h)),
                  pl.BlockSpec((1, seq, 1), lambda b, h: (b, 0, 0)),
                  pl.BlockSpec((1, hd), lambda b, h: (0, 0)),
                  pl.BlockSpec((1, hd), lambda b, h: (0, 0)),
                  pl.BlockSpec((1, hd), lambda b, h: (0, 0))],
        out_specs=pl.BlockSpec((seq, hd), lambda b, h: (b, h)),
        out_shape=jax.ShapeDtypeStruct((t, D_ATTN), BF16),
        scratch_shapes=[scr] * 9,
        compiler_params=_cparams(("parallel", "arbitrary")),
        name="attention",
    )(proj, proj, proj, pos3, freq, qw, kw)


def _split3(v):
    hi = v.astype(BF16)
    r1 = v - hi.astype(F32)
    mid = r1.astype(BF16)
    lo = (r1 - mid.astype(F32)).astype(BF16)
    return hi, mid, lo


def _ssd_kernel(xs_ref, bm_ref, cm_ref, z_ref, dtr_ref, cwx_ref, cwb_ref, cwc_ref,
                cbx_ref, cbb_ref, cbc_ref, dtb_ref, alog_ref, dsk_ref, nw_ref, o_ref,
                xpad, x_s, b_s, c_s, dt_s, da_s, dtb_s, daf_s):
    seq = xs_ref.shape[0]
    ck = SSD_CHUNK
    g = pl.program_id(1)
    pad = SUBLANES

    def conv_silu(src_ref, w_ref, b_ref, dst):
        xpad[0:pad, :] = jnp.zeros((pad, LANES), F32)
        xpad[pad:, :] = src_ref[...].astype(F32)
        y = b_ref[...] + jnp.zeros((seq, LANES), F32)
        for j in range(CONV_WIDTH):
            y = y + w_ref[j:j + 1, :] * xpad[pl.ds(pad - (CONV_WIDTH - 1) + j, seq), :]
        dst[...] = y * jax.nn.sigmoid(y)

    conv_silu(xs_ref, cwx_ref, cbx_ref, x_s)
    conv_silu(bm_ref, cwb_ref, cbb_ref, b_s)
    conv_silu(cm_ref, cwc_ref, cbc_ref, c_s)

    @pl.when(g == 0)
    def _():
        xr = dtr_ref[...] + dtb_ref[...]
        dt = jnp.maximum(xr, 0.0) + jnp.log1p(jnp.exp(-jnp.abs(xr)))
        dt_s[...] = dt
        da_s[...] = dt * (-jnp.exp(alog_ref[...]))

    def select(v, sel):
        hi, mid, lo = _split3(v)
        return (jnp.dot(hi, sel, preferred_element_type=F32)
                + jnp.dot(mid, sel, preferred_element_type=F32)
                + jnp.dot(lo, sel, preferred_element_type=F32))

    hrow = lax.broadcasted_iota(jnp.int32, (LANES, 2 * LANES), 0)
    hlane = lax.broadcasted_iota(jnp.int32, (LANES, 2 * LANES), 1)
    brow = lax.broadcasted_iota(jnp.int32, (LANES, LANES), 0)
    blane = lax.broadcasted_iota(jnp.int32, (LANES, LANES), 1)
    sel_b = (brow == 2 * g + (blane >= SSM_HEAD_DIM).astype(jnp.int32)).astype(BF16)
    sel_f = (hrow == 2 * g + (hlane >= LANES).astype(jnp.int32)).astype(BF16)
    dtb_s[...] = select(dt_s[...], sel_b)
    daf_s[...] = select(da_s[...], sel_f)

    li = lax.broadcasted_iota(jnp.int32, (ck, ck), 0)
    si = lax.broadcasted_iota(jnp.int32, (ck, ck), 1)
    causal = si <= li
    eye = si == li
    tril_bf = causal.astype(BF16)
    head1 = lax.broadcasted_iota(jnp.int32, (ck, LANES), 1) >= SSM_HEAD_DIM
    head1_row = lax.broadcasted_iota(jnp.int32, (1, LANES), 1) >= SSM_HEAD_DIM

    def chunk(c, prev):
        r = pl.ds(pl.multiple_of(c * ck, ck), ck)
        xc = x_s[r, :]
        bc = b_s[r, :].astype(BF16)
        cc = c_s[r, :].astype(BF16)
        hi, mid, lo = _split3(daf_s[r, :])
        acs2 = (jnp.dot(tril_bf, hi, preferred_element_type=F32)
                + jnp.dot(tril_bf, mid, preferred_element_type=F32)
                + jnp.dot(tril_bf, lo, preferred_element_type=F32))
        acs_f = (acs2[:, :LANES], acs2[:, LANES:])
        acs_b = jnp.where(head1, acs_f[1], acs_f[0])
        last_b = jnp.where(head1_row, acs_f[1][ck - 1:ck, :], acs_f[0][ck - 1:ck, :])
        cb = lax.dot_general(cc, bc, (((1,), (1,)), ((), ())), preferred_element_type=F32)
        xdt = xc * dtb_s[r, :]
        scores = []
        for k in range(2):
            acs_row = jnp.sum(jnp.where(eye, acs_f[k], 0.0), axis=0, keepdims=True)
            decay = jnp.exp(jnp.where(causal, acs_f[k] - acs_row, -jnp.inf))
            scores.append((cb * decay).astype(BF16))
        xdt2 = jnp.concatenate([jnp.where(head1, 0.0, xdt).astype(BF16),
                                jnp.where(head1, xdt, 0.0).astype(BF16)], axis=0)
        y_diag = jnp.dot(jnp.concatenate(scores, axis=1), xdt2, preferred_element_type=F32)
        y_off = jnp.dot(cc, prev.astype(BF16), preferred_element_type=F32) * jnp.exp(acs_b)
        y = y_diag + y_off + dsk_ref[...] * xc
        zc = z_ref[r, :].astype(F32)
        y = y * (zc * jax.nn.sigmoid(zc))
        y = y * lax.rsqrt(jnp.mean(y * y, axis=-1, keepdims=True) + EPS) * nw_ref[...]
        o_ref[r, :] = y.astype(BF16)
        w = (xdt * jnp.exp(last_b - acs_b)).astype(BF16)
        st = lax.dot_general(bc, w, (((0,), (0,)), ((), ())), preferred_element_type=F32)
        return prev * jnp.exp(last_b) + st

    lax.fori_loop(0, seq // ck, chunk, jnp.zeros((SSM_STATE, LANES), F32), unroll=2)


def _ssd(proj, dt_raw, conv_w, conv_b, dtb, alog, dsk, nw, batch, seq):
    t = proj.shape[0]
    gs = SSM_GROUPS
    col = lambda off: (lambda b, g: (b, off + g))
    wcol = lambda off: (lambda b, g: (0, off + g))
    zq, xq, bq, cq = 3 * D_ATTN // LANES, 4 * D_ATTN // LANES, 5 * D_ATTN // LANES, 6 * D_ATTN // LANES
    const = lambda b, g: (0, 0)
    scr = pltpu.VMEM((seq, LANES), F32)
    return pl.pallas_call(
        _ssd_kernel,
        grid=(batch, gs),
        in_specs=[pl.BlockSpec((seq, LANES), col(xq)),
                  pl.BlockSpec((seq, LANES), col(bq)),
                  pl.BlockSpec((seq, LANES), col(cq)),
                  pl.BlockSpec((seq, LANES), col(zq)),
                  pl.BlockSpec((seq, LANES), lambda b, g: (b, 0)),
                  pl.BlockSpec((CONV_WIDTH, LANES), wcol(0)),
                  pl.BlockSpec((CONV_WIDTH, LANES), wcol(gs)),
                  pl.BlockSpec((CONV_WIDTH, LANES), wcol(2 * gs)),
                  pl.BlockSpec((1, LANES), wcol(0)),
                  pl.BlockSpec((1, LANES), wcol(gs)),
                  pl.BlockSpec((1, LANES), wcol(2 * gs)),
                  pl.BlockSpec((1, LANES), const),
                  pl.BlockSpec((1, LANES), const),
                  pl.BlockSpec((1, LANES), lambda b, g: (0, g)),
                  pl.BlockSpec((1, LANES), lambda b, g: (0, g))],
        out_specs=pl.BlockSpec((seq, LANES), lambda b, g: (b, g)),
        out_shape=jax.ShapeDtypeStruct((t, D_SSM), BF16),
        scratch_shapes=[pltpu.VMEM((seq + SUBLANES, LANES), F32), scr, scr, scr, scr, scr, scr,
                        pltpu.VMEM((seq, 2 * LANES), F32)],
        compiler_params=_cparams(("arbitrary", "arbitrary")),
        name="ssd",
    )(proj, proj, proj, proj, dt_raw, conv_w, conv_w, conv_w, conv_b, conv_b, conv_b,
      dtb, alog, dsk, nw)


def _out_router_kernel(ya_ref, ys_ref, wa_ref, ws_ref, x_ref, mod_ref, nw_ref, rw_ref, rb_ref,
                       x1_ref, h2_ref, eid_ref, ew_ref, rank_ref, cnt_ref, cnt_s):
    tm = x_ref.shape[0]
    mix = (jnp.dot(ya_ref[...], wa_ref[...], preferred_element_type=F32)
           + jnp.dot(ys_ref[...], ws_ref[...], preferred_element_type=F32))
    x1 = x_ref[...] + mod_ref[0, 2:3, :] * mix
    x1_ref[...] = x1
    h2 = _modulated_norm(x1, nw_ref[...], mod_ref[0, 3:4, :], mod_ref[0, 4:5, :])
    h2_ref[...] = h2

    h_hi = h2.astype(BF16)
    h_lo = (h2 - h_hi.astype(F32)).astype(BF16)
    rw = rw_ref[...]
    w_hi = rw.astype(BF16)
    w_lo = (rw - w_hi.astype(F32)).astype(BF16)
    nt = (((1,), (1,)), ((), ()))
    logits = (lax.dot_general(w_hi, h_hi, nt, preferred_element_type=F32)
              + lax.dot_general(w_hi, h_lo, nt, preferred_element_type=F32)
              + lax.dot_general(w_lo, h_hi, nt, preferred_element_type=F32)) + rb_ref[...]

    ng, ne = N_EXPERT_GROUPS, EXPERTS_PER_GROUP
    iota = lax.broadcasted_iota(jnp.int32, (ne, tm), 0)

    def argmax0(v):
        mx = jnp.max(v, axis=0, keepdims=True)
        idx = jnp.min(jnp.where(v == mx, iota, ne), axis=0, keepdims=True)
        return mx, idx

    gl = logits[N_EXPERTS:N_EXPERTS + ng, :]
    gmax, gidx = argmax0(gl)
    g_gate = 1.0 / jnp.sum(jnp.exp(gl - gmax), axis=0, keepdims=True)
    e_in = jnp.zeros((ne, tm), F32)
    for gi in range(ng):
        e_in = e_in + jnp.where(gidx == gi, logits[gi * ne:(gi + 1) * ne, :], 0.0)
    m1, i1 = argmax0(e_in)
    m2, i2 = argmax0(jnp.where(iota == i1, -jnp.inf, e_in))
    p2 = jnp.exp(m2 - m1)
    den = 1.0 + p2
    e0 = gidx * ne + i1
    e1 = gidx * ne + i2
    eid_ref[0, 0:1, :] = e0
    eid_ref[0, 1:2, :] = e1
    ew_ref[0:1, :] = (1.0 / den) * g_gate
    ew_ref[1:2, :] = (p2 / den) * g_gate

    @pl.when(pl.program_id(0) == 0)
    def _():
        cnt_s[...] = jnp.zeros(cnt_s.shape, F32)

    eiota = lax.broadcasted_iota(jnp.int32, (N_EXPERTS, tm), 0)
    oh0 = eiota == e0
    oh1 = eiota == e1
    both = jnp.logical_or(oh0, oh1)
    ti = lax.broadcasted_iota(jnp.int32, (tm, tm), 0)
    tj = lax.broadcasted_iota(jnp.int32, (tm, tm), 1)
    earlier = (ti < tj).astype(BF16)
    before = cnt_s[...] + jnp.dot(both.astype(BF16), earlier, preferred_element_type=F32)
    rank_ref[0, 0:1, :] = jnp.sum(jnp.where(oh0, before, 0.0), axis=0, keepdims=True).astype(jnp.int32)
    rank_ref[0, 1:2, :] = jnp.sum(jnp.where(oh1, before, 0.0), axis=0, keepdims=True).astype(jnp.int32)
    total = cnt_s[...] + jnp.sum(both.astype(F32), axis=1, keepdims=True)
    cnt_s[...] = total
    cnt_ref[...] = total.astype(jnp.int32)


def _out_router(ya, ys, wa, ws, x2, mod3, norm_w, rw, rb, seq):
    t, d = x2.shape
    tm = OUT_TM
    const = lambda i: (0, 0)
    return pl.pallas_call(
        _out_router_kernel,
        grid=(t // tm,),
        in_specs=[pl.BlockSpec((tm, D_ATTN), lambda i: (i, 0)),
                  pl.BlockSpec((tm, D_SSM), lambda i: (i, 0)),
                  pl.BlockSpec((D_ATTN, d), const, pipeline_mode=pl.Buffered(1)),
                  pl.BlockSpec((D_SSM, d), const, pipeline_mode=pl.Buffered(1)),
                  pl.BlockSpec((tm, d), lambda i: (i, 0)),
                  pl.BlockSpec((1, 6, d), lambda i: (i * tm // seq, 0, 0)),
                  pl.BlockSpec((1, d), const),
                  pl.BlockSpec((LANES, d), const, pipeline_mode=pl.Buffered(1)),
                  pl.BlockSpec((LANES, 1), const)],
        out_specs=[pl.BlockSpec((tm, d), lambda i: (i, 0)),
                   pl.BlockSpec((tm, d), lambda i: (i, 0)),
                   pl.BlockSpec((1, 2, tm), lambda i: (i, 0, 0)),
                   pl.BlockSpec((2, tm), lambda i: (0, i)),
                   pl.BlockSpec((1, 2, tm), lambda i: (i, 0, 0)),
                   pl.BlockSpec((N_EXPERTS, 1), const)],
        out_shape=[jax.ShapeDtypeStruct((t, d), F32),
                   jax.ShapeDtypeStruct((t, d), F32),
                   jax.ShapeDtypeStruct((t // tm, 2, tm), jnp.int32),
                   jax.ShapeDtypeStruct((2, t), F32),
                   jax.ShapeDtypeStruct((t // tm, 2, tm), jnp.int32),
                   jax.ShapeDtypeStruct((N_EXPERTS, 1), jnp.int32)],
        scratch_shapes=[pltpu.VMEM((N_EXPERTS, 1), F32)],
        compiler_params=_cparams(("arbitrary",)),
        name="out_router",
    )(ya, ys, wa, ws, x2, mod3, norm_w, rw, rb)


def _row_of(pstart_ref, eid_ref, rank_ref, k, t):
    return pstart_ref[eid_ref[0, k, t]] + rank_ref[0, k, t]


def _scatter_kernel(pstart_ref, zstart_ref, eid_ref, rank_ref, h_ref, xd_hbm, sem, zbuf, zsem):
    tm = h_ref.shape[0]

    @pl.when(pl.program_id(0) == 0)
    def _():
        zbuf[...] = jnp.zeros(zbuf.shape, F32)

        def zero_copy(j):
            start = pl.multiple_of(jnp.maximum(zstart_ref[j], 0), MOE_ROWS)
            return pltpu.make_async_copy(zbuf, xd_hbm.at[pl.ds(start, MOE_ROWS)], zsem)

        def z_issue(j, carry):
            @pl.when(zstart_ref[j] >= 0)
            def _():
                zero_copy(j).start()
            return carry

        def z_wait(j, carry):
            @pl.when(zstart_ref[j] >= 0)
            def _():
                zero_copy(j).wait()
            return carry

        lax.fori_loop(0, zstart_ref.shape[0], z_issue, 0)
        lax.fori_loop(0, zstart_ref.shape[0], z_wait, 0)

    def issue(t, carry):
        for k in range(2):
            row = _row_of(pstart_ref, eid_ref, rank_ref, k, t)
            pltpu.make_async_copy(h_ref.at[pl.ds(t, 1)], xd_hbm.at[pl.ds(row, 1)], sem).start()
        return carry

    lax.fori_loop(0, tm, issue, 0, unroll=8)
    for _ in range(2):
        pltpu.make_async_copy(h_ref, xd_hbm.at[pl.ds(0, tm)], sem).wait()


def _scatter_rows(pstart, zstart, eid3, rank3, h2, n_rows):
    t, d = h2.shape
    n_tiles, _, tm = eid3.shape
    smem = lambda: pl.BlockSpec((1, 2, tm), lambda i, ps, zs: (i, 0, 0), memory_space=pltpu.SMEM)
    grid_spec = pltpu.PrefetchScalarGridSpec(
        num_scalar_prefetch=2,
        grid=(n_tiles,),
        in_specs=[smem(), smem(), pl.BlockSpec((tm, d), lambda i, ps, zs: (i, 0))],
        out_specs=pl.BlockSpec(memory_space=pl.ANY),
        scratch_shapes=[pltpu.SemaphoreType.DMA(()), pltpu.VMEM((MOE_ROWS, d), F32),
                        pltpu.SemaphoreType.DMA(())],
    )
    return pl.pallas_call(
        _scatter_kernel,
        grid_spec=grid_spec,
        out_shape=jax.ShapeDtypeStruct((n_rows, d), F32),
        compiler_params=_cparams(("arbitrary",)),
        name="scatter_rows",
    )(pstart, zstart, eid3, rank3, h2)


def _experts_kernel(be_ref, nv_ref, x_ref, wg_ref, wu_ref, wd_ref, o_ref, wg_s, wu_s, wd_s):
    i = pl.program_id(0)
    n_valid = nv_ref[i]
    changed =jnp.logical_or(i == 0, be_ref[i] != be_ref[jnp.maximum(i - 1, 0)])

    @pl.when(jnp.logical_and(changed, n_valid > 0))
    def _():
        wg_s[...] = wg_ref[0].astype(BF16)
        wu_s[...] = wu_ref[0].astype(BF16)
        wd_s[...] = wd_ref[0].astype(BF16)

    @pl.when(n_valid > 0)
    def _():
        xb = x_ref[...].astype(BF16)
        gt = jnp.dot(xb, wg_s[...], preferred_element_type=F32)
        up = jnp.dot(xb, wu_s[...], preferred_element_type=F32)
        hid = (gt * jax.nn.sigmoid(gt)) * up
        o_ref[...] = jnp.dot(hid.astype(BF16), wd_s[...], preferred_element_type=F32)

    @pl.when(n_valid == 0)
    def _():
        o_ref[...] = jnp.zeros(o_ref.shape, F32)


def _experts(block_e, n_valid, xd, w_gate, w_up, w_down):
    rows = MOE_ROWS
    n_blocks = xd.shape[0] // rows
    d = xd.shape[1]
    de = w_gate.shape[2]
    grid_spec = pltpu.PrefetchScalarGridSpec(
        num_scalar_prefetch=2,
        grid=(n_blocks,),
        in_specs=[pl.BlockSpec((rows, d), lambda i, be, nv: (i, 0)),
                  pl.BlockSpec((1, d, de), lambda i, be, nv: (be[i], 0, 0)),
                  pl.BlockSpec((1, d, de), lambda i, be, nv: (be[i], 0, 0)),
                  pl.BlockSpec((1, de, d), lambda i, be, nv: (be[i], 0, 0))],
        out_specs=pl.BlockSpec((rows, d), lambda i, be, nv: (i, 0)),
        scratch_shapes=[pltpu.VMEM((d, de), BF16),
                        pltpu.VMEM((d, de), BF16),
                        pltpu.VMEM((de, d), BF16)],
    )
    return pl.pallas_call(
        _experts_kernel,
        grid_spec=grid_spec,
        out_shape=jax.ShapeDtypeStruct((n_blocks * rows, d), F32),
        compiler_params=_cparams(("arbitrary",)),
        name="experts",
    )(block_e, n_valid, xd, w_gate, w_up, w_down)


def _combine_kernel(pstart_ref, eid_ref, rank_ref, eidn_ref, rankn_ref, yd_hbm, x1_ref, mod_ref, ew_ref,
                    o_ref, ybuf, sems):
    i = pl.program_id(0)
    n = pl.num_programs(0)
    tm = x1_ref.shape[0]
    slot = i % 2

    def gather(e_ref, r_ref, s):
        def issue(t, carry):
            for k in range(2):
                row = _row_of(pstart_ref, e_ref, r_ref, k, t)
                pltpu.make_async_copy(yd_hbm.at[pl.ds(row, 1)], ybuf.at[s, pl.ds(k * tm + t, 1)],
                                      sems.at[s]).start()
            return carry
        lax.fori_loop(0, tm, issue, 0, unroll=8)

    @pl.when(i == 0)
    def _():
        gather(eid_ref, rank_ref, 0)

    @pl.when(i + 1 < n)
    def _():
        gather(eidn_ref, rankn_ref, 1 - slot)

    pltpu.make_async_copy(yd_hbm.at[pl.ds(0, 2 * tm)], ybuf.at[slot], sems.at[slot]).wait()
    y0 = ybuf[slot, 0:tm, :]
    y1 = ybuf[slot, tm:2 * tm, :]
    moe = ew_ref[:, 0:1] * y0 + ew_ref[:, 1:2] * y1
    o_ref[...] = x1_ref[...] + mod_ref[0, 5:6, :] * moe


def _combine(pstart, eid3, rank3, yd, x1, mod3, ew_t, seq):
    t, d = x1.shape
    n, _, tm = eid3.shape
    cur = lambda i, ps: (i, 0, 0)
    nxt = lambda i, ps: (jnp.minimum(i + 1, n - 1), 0, 0)
    smem = lambda im: pl.BlockSpec((1, 2, tm), im, memory_space=pltpu.SMEM)
    grid_spec = pltpu.PrefetchScalarGridSpec(
        num_scalar_prefetch=1,
        grid=(n,),
        in_specs=[smem(cur), smem(cur), smem(nxt), smem(nxt),
                  pl.BlockSpec(memory_space=pl.ANY),
                  pl.BlockSpec((tm, d), lambda i, ps: (i, 0)),
                  pl.BlockSpec((1, 6, d), lambda i, ps: (i * tm // seq, 0, 0)),
                  pl.BlockSpec((tm, 2), lambda i, ps: (i, 0))],
        out_specs=pl.BlockSpec((tm, d), lambda i, ps: (i, 0)),
        scratch_shapes=[pltpu.VMEM((2, 2 * tm, d), F32), pltpu.SemaphoreType.DMA((2,))],
    )
    return pl.pallas_call(
        _combine_kernel,
        grid_spec=grid_spec,
        out_shape=jax.ShapeDtypeStruct((t, d), F32),
        compiler_params=_cparams(("arbitrary",)),
        name="combine",
    )(pstart, eid3, rank3, eid3, rank3, yd, x1, mod3, ew_t)


def _layout(counts, n_assign):
    rows = MOE_ROWS
    n_blocks = (n_assign + N_EXPERTS * (rows - 1)) // rows
    padded = (counts + rows - 1) // rows * rows
    pend = jnp.cumsum(padded)
    pstart = pend - padded
    bstart = jnp.arange(n_blocks, dtype=jnp.int32) * rows
    block_e = jnp.minimum(jnp.sum((pend[None, :] <= bstart[:, None]).astype(jnp.int32), axis=1),
                          N_EXPERTS - 1)
    n_valid = jnp.clip(pstart[block_e] + counts[block_e] - bstart, 0, rows)
    n_valid = jnp.where(bstart < pend[-1], n_valid, 0)
    min_used = n_assign // rows
    tail = pend[-1] + jnp.arange(n_blocks - min_used, dtype=jnp.int32) * rows
    zstart = jnp.concatenate([jnp.where(counts > 0, pend - rows, -1),
                              jnp.where(tail < n_blocks * rows, tail, -1)])
    return (pstart.astype(jnp.int32), zstart.astype(jnp.int32), block_e.astype(jnp.int32),
            n_valid.astype(jnp.int32), n_blocks * rows)


def kernel(x, c, positions, ada_w, ada_b, norm1_w, w_in, q_norm_w, k_norm_w, conv_w, conv_b,
           dt_bias, a_log, d_skip, ssm_norm_w, w_out, norm2_w, router_group_w, router_group_b,
           router_expert_w, router_expert_b, w_gate, w_up, w_down):
    batch, seq, d = x.shape
    t = batch * seq
    depth = ada_w.shape[0]
    nh_pad = LANES - N_SSM_HEADS
    inv_freq = ROPE_THETA ** (-(jnp.arange(0, ROT_DIM, 2, dtype=F32) / ROT_DIM))
    freq = jnp.concatenate([inv_freq, inv_freq, jnp.zeros((HEAD_DIM - ROT_DIM,), F32)]).reshape(1, HEAD_DIM)
    pos3 = positions.reshape(batch, seq, 1)
    xc = x.reshape(t, d)
    for l in range(depth):
        mod3 = _adaln(c, ada_w[l], ada_b[l]).reshape(batch, 6, d)
        w_main = w_in[l][:, :N_MAIN].astype(BF16)
        w_dt = jnp.pad(w_in[l][:, N_MAIN:], ((0, 0), (0, nh_pad))).astype(BF16)
        proj, dt_raw = _in_proj(xc, mod3, norm1_w[l].reshape(1, d), w_main, w_dt, seq)
        y_attn = _attention(proj, pos3, freq, q_norm_w[l].reshape(1, HEAD_DIM),
                            k_norm_w[l].reshape(1, HEAD_DIM), batch, seq)
        y_ssm = _ssd(proj, dt_raw, conv_w[l], conv_b[l].reshape(1, -1),
                     jnp.pad(dt_bias[l], (0, nh_pad)).reshape(1, LANES),
                     jnp.pad(a_log[l], (0, nh_pad)).reshape(1, LANES),
                     jnp.repeat(d_skip[l], SSM_HEAD_DIM).reshape(1, D_SSM),
                     ssm_norm_w[l].reshape(1, D_SSM), batch, seq)
        rw = jnp.concatenate([router_expert_w[l].T, router_group_w[l].T,
                              jnp.zeros((LANES - N_EXPERTS - N_EXPERT_GROUPS, d), F32)])
        rb = jnp.concatenate([router_expert_b[l], router_group_b[l],
                              jnp.zeros((LANES - N_EXPERTS - N_EXPERT_GROUPS,), F32)]).reshape(LANES, 1)
        wo = w_out[l].astype(BF16)
        x1, h2, eid3, ew, rank3, counts = _out_router(y_attn, y_ssm, wo[:D_ATTN], wo[D_ATTN:], xc, mod3,
                                                      norm2_w[l].reshape(1, d), rw, rb, seq)
        pstart, zstart, block_e, n_valid, n_rows = _layout(counts.reshape(N_EXPERTS), 2 * t)
        xd = _scatter_rows(pstart, zstart, eid3, rank3, h2, n_rows)
        yd = _experts(block_e, n_valid, xd, w_gate[l], w_up[l], w_down[l])
        xc = _combine(pstart, eid3, rank3, yd, x1, mod3, ew.T, seq)
    return xc.reshape(batch, seq, d)
```

```python
import jax
import jax.numpy as jnp
from jax import lax
from jax.experimental import pallas as pl
from jax.experimental.pallas import tpu as pltpu

F32 = jnp.float32
BF16 = jnp.bfloat16

D_MODEL = 2048
D_ATTN = 1024
D_SSM = 1024
HEAD_DIM = 128
N_ATTN_HEADS = 8
ROT_DIM = 32
ROPE_THETA = 500000.0
ATTN_BLOCK = 128
DILATIONS = (1, 4, 16)
SSM_HEAD_DIM = 64
N_SSM_HEADS = 16
SSM_GROUPS = 8
SSM_STATE = 128
CONV_WIDTH = 4
SSD_CHUNK = 128
N_MAIN = 3 * D_ATTN + D_SSM + D_SSM + 2 * SSM_GROUPS * SSM_STATE
N_EXPERT_GROUPS = 8
EXPERTS_PER_GROUP = 8
N_EXPERTS = 64
D_EXPERT = 512
EPS = 1e-6

LANES = 128
SUBLANES = 8
VMEM_LIMIT = 56 * 1024 * 1024

ADALN_TN = 1024
INPROJ_TM = 1024
INPROJ_TN = 1024
ATTN_GROUP = 5
OUT_TM = 512
MOE_ROWS = 256


def _cparams(sem):
    return pltpu.CompilerParams(dimension_semantics=sem, vmem_limit_bytes=VMEM_LIMIT)


def _pack_bf16_pair(x):
    k = x.shape[1] // 2
    lo = lax.bitcast_convert_type(x[:, :k].astype(BF16).astype(F32), jnp.uint32)
    hi = lax.bitcast_convert_type(x[:, k:].astype(BF16).astype(F32), jnp.uint32)
    return (lo >> 16) | (hi & jnp.uint32(0xFFFF0000))


def _unpack_bf16_pair(p):
    lo = lax.bitcast_convert_type(p << 16, F32)
    hi = lax.bitcast_convert_type(p & jnp.uint32(0xFFFF0000), F32)
    return lo, hi


def _adaln_kernel(c_ref, w_ref, b_ref, o_ref):
    c = c_ref[...]
    ca = (c * jax.nn.sigmoid(c)).astype(BF16)
    o_ref[...] = jnp.dot(ca, w_ref[...].astype(BF16), preferred_element_type=F32) + b_ref[...]


def _adaln(c, ada_w, ada_b):
    b, d = c.shape
    n = ada_w.shape[1]
    return pl.pallas_call(
        _adaln_kernel,
        grid=(n // ADALN_TN,),
        in_specs=[pl.BlockSpec((b, d), lambda j: (0, 0)),
                  pl.BlockSpec((d, ADALN_TN), lambda j: (0, j)),
                  pl.BlockSpec((1, ADALN_TN), lambda j: (0, j))],
        out_specs=pl.BlockSpec((b, ADALN_TN), lambda j: (0, j)),
        out_shape=jax.ShapeDtypeStruct((b, n), F32),
        compiler_params=_cparams(("arbitrary",)),
        name="adaln",
    )(c, ada_w, ada_b.reshape(1, n))


def _modulated_norm(x, nw, shift, scale):
    y = x * lax.rsqrt(jnp.mean(x * x, axis=-1, keepdims=True) + EPS) * nw
    return y * (1.0 + scale) + shift


def _inproj_kernel(x_ref, mod_ref, nw_ref, w_ref, wdt_ref, o_ref, dt_ref, h_scr):
    @pl.when(pl.program_id(1) == 0)
    def _():
        h = _modulated_norm(x_ref[...], nw_ref[...], mod_ref[0, 0:1, :], mod_ref[0, 1:2, :])
        h_scr[...] = h.astype(BF16)
        dt_ref[...] = jnp.dot(h_scr[...], wdt_ref[...], preferred_element_type=F32)

    o_ref[...] = jnp.dot(h_scr[...], w_ref[...], preferred_element_type=F32).astype(BF16)


def _in_proj(x2, mod3, norm_w, w_main, w_dt, seq):
    t, d = x2.shape
    tm, tn = INPROJ_TM, INPROJ_TN
    return pl.pallas_call(
        _inproj_kernel,
        grid=(t // tm, N_MAIN // tn),
        in_specs=[pl.BlockSpec((tm, d), lambda i, j: (i, 0)),
                  pl.BlockSpec((1, 6, d), lambda i, j: (i * tm // seq, 0, 0)),
                  pl.BlockSpec((1, d), lambda i, j: (0, 0)),
                  pl.BlockSpec((d, tn), lambda i, j: (0, j)),
                  pl.BlockSpec((d, LANES), lambda i, j: (0, 0))],
        out_specs=[pl.BlockSpec((tm, tn), lambda i, j: (i, j)),
                   pl.BlockSpec((tm, LANES), lambda i, j: (i, 0))],
        out_shape=[jax.ShapeDtypeStruct((t, N_MAIN), BF16),
                   jax.ShapeDtypeStruct((t, LANES), F32)],
        scratch_shapes=[pltpu.VMEM((tm, d), BF16)],
        compiler_params=_cparams(("parallel", "arbitrary")),
        name="in_proj",
    )(x2, mod3, norm_w, w_main, w_dt)


def _attn_kernel(q_ref, k_ref, v_ref, pos_ref, freq_ref, qw_ref, kw_ref, o_ref,
                 cos_s, sa_s, sb_s, qs, ks, vs, acc_s, m_s, l_s):
    seq = q_ref.shape[0]
    blk = ATTN_BLOCK
    half = ROT_DIM // 2

    @pl.when(pl.program_id(1) == 0)
    def _():
        ang = pos_ref[0].astype(F32) * freq_ref[...]
        lane = lax.broadcasted_iota(jnp.int32, (seq, LANES), 1)
        sn = jnp.sin(ang)
        cos_s[...] = jnp.cos(ang)
        sa_s[...] = jnp.where(lane < half, -sn, 0.0)
        sb_s[...] = jnp.where((lane >= half) & (lane < ROT_DIM), sn, 0.0)

    def norm_rope(t_ref, w_ref):
        t = t_ref[...].astype(F32)
        y = t * lax.rsqrt(jnp.mean(t * t, axis=-1, keepdims=True) + EPS) * w_ref[...]
        return (y * cos_s[...] + pltpu.roll(y, LANES - half, 1) * sa_s[...]
                + pltpu.roll(y, half, 1) * sb_s[...])

    qs[...] = norm_rope(q_ref, qw_ref)
    ks[...] = norm_rope(k_ref, kw_ref)
    vs[...] = v_ref[...].astype(F32)

    qi = lax.broadcasted_iota(jnp.int32, (blk, blk), 0)
    kj = lax.broadcasted_iota(jnp.int32, (blk, blk), 1)
    mask_cur = kj <= qi
    mask_prev = kj >= qi
    mask_both = jnp.concatenate([mask_prev, mask_cur], axis=1)
    ones_blk = jnp.ones((blk, LANES), BF16)
    scale = HEAD_DIM ** -0.5
    nt = (((1,), (1,)), ((), ()))

    def rows(start, d):
        return pl.ds(start, blk) if d == 1 else pl.ds(start, blk, stride=d)

    def group(starts, d, with_prev, first_pattern):
        r_cur = [rows(s, d) for s in starts]
        qb = [qs[r, :].astype(BF16) for r in r_cur]
        kb = [ks[r, :].astype(BF16) for r in r_cur]
        vb = [jnp.concatenate([vs[r, :].astype(BF16), ones_blk], axis=1) for r in r_cur]
        if with_prev:
            r_prev = [rows(s - blk * d, d) for s in starts]
            kb = [jnp.concatenate([ks[rp, :].astype(BF16), k], axis=0) for rp, k in zip(r_prev, kb)]
            vb = [jnp.concatenate([jnp.concatenate([vs[rp, :].astype(BF16), ones_blk], axis=1), v], axis=0)
                  for rp, v in zip(r_prev, vb)]
        mask = mask_both if with_prev else mask_cur
        sc = [jnp.where(mask, lax.dot_general(q, k, nt, preferred_element_type=F32) * scale, -jnp.inf)
              for q, k in zip(qb, kb)]
        if with_prev:
            mx = [jnp.max(jnp.maximum(x[:, :blk], x[:, blk:]), axis=-1, keepdims=True) for x in sc]
        else:
            mx = [jnp.max(x, axis=-1, keepdims=True) for x in sc]
        pr = [jnp.exp(x - m).astype(BF16) for x, m in zip(sc, mx)]
        ov = [jnp.dot(p, v, preferred_element_type=F32) for p, v in zip(pr, vb)]
        mb = [jnp.broadcast_to(m, (blk, LANES)) for m in mx]
        if first_pattern:
            for r, m, o in zip(r_cur, mb, ov):
                m_s[r, :] = m
                l_s[r, :] = o[:, LANES:]
                acc_s[r, :] = o[:, :LANES]
        else:
            old = [(m_s[r, :], l_s[r, :], acc_s[r, :]) for r in r_cur]
            new = []
            for (m_old, l_old, acc_old), m, o in zip(old, mb, ov):
                m_new = jnp.maximum(m_old, m)
                a_old = jnp.exp(m_old - m_new)
                a_new = jnp.exp(m - m_new)
                new.append((m_new, a_old * l_old + a_new * o[:, LANES:],
                            a_old * acc_old + a_new * o[:, :LANES]))
            for r, (m_new, l_new, acc_new) in zip(r_cur, new):
                m_s[r, :] = m_new
                l_s[r, :] = l_new
                acc_s[r, :] = acc_new

    def loop(n_iter, starts_of, d, with_prev, first_pattern):
        def body(i, carry):
            group(starts_of(i), d, with_prev, first_pattern)
            return carry
        lax.fori_loop(0, n_iter, body, 0)

    for d in DILATIONS:
        n_blk = seq // (d * blk)
        first_pattern = d == DILATIONS[0]
        span = blk * d
        if d == 1:
            group([0], d, False, first_pattern)
        else:
            loop(d // 4, lambda i: [4 * i + j for j in range(4)], d, False, first_pattern)
        if d == 1:
            loop((n_blk - 1) // ATTN_GROUP,
                 lambda i: [(1 + ATTN_GROUP * i + j) * span for j in range(ATTN_GROUP)],
                 d, True, first_pattern)
        elif n_blk > 1:
            loop(d // 2, lambda i: [2 * i + j + n * span for j in range(2) for n in range(1, n_blk)],
                 d, True, first_pattern)

    o_ref[...] = (acc_s[...] / l_s[...]).astype(BF16)


def _attention(proj, pos3, freq, qw, kw, batch, seq):
    t = proj.shape[0]
    hd = HEAD_DIM
    nh = N_ATTN_HEADS
    assert seq % (ATTN_BLOCK * max(DILATIONS)) == 0 and (seq // ATTN_BLOCK - 1) % ATTN_GROUP == 0
    scr = pltpu.VMEM((seq, hd), F32)
    return pl.pallas_call(
        _attn_kernel,
        grid=(batch, nh),
        in_specs=[pl.BlockSpec((seq, hd), lambda b, h: (b, h)),
                  pl.BlockSpec((seq, hd), lambda b, h: (b, nh + h)),
                  pl.BlockSpec((seq, hd), lambda b, h: (b, 2 * nh + h)),
                  pl.BlockSpec((1, seq, 1), lambda b, h: (b, 0, 0)),
                  pl.BlockSpec((1, hd), lambda b, h: (0, 0)),
                  pl.BlockSpec((1, hd), lambda b, h: (0, 0)),
                  pl.BlockSpec((1, hd), lambda b, h: (0, 0))],
        out_specs=pl.BlockSpec((seq, hd), lambda b, h: (b, h)),
        out_shape=jax.ShapeDtypeStruct((t, D_ATTN), BF16),
        scratch_shapes=[scr] * 9,
        compiler_params=_cparams(("parallel", "arbitrary")),
        name="attention",
    )(proj, proj, proj, pos3, freq, qw, kw)


def _split3(v):
    hi = v.astype(BF16)
    r1 = v - hi.astype(F32)
    mid = r1.astype(BF16)
    lo = (r1 - mid.astype(F32)).astype(BF16)
    return hi, mid, lo


def _ssd_kernel(xs_ref, bm_ref, cm_ref, z_ref, dtr_ref, cwx_ref, cwb_ref, cwc_ref,
                cbx_ref, cbb_ref, cbc_ref, dtb_ref, alog_ref, dsk_ref, nw_ref, o_ref,
                xpad, x_s, b_s, c_s, dt_s, da_s, dtb_s, daf_s):
    seq = xs_ref.shape[0]
    ck = SSD_CHUNK
    g = pl.program_id(1)
    pad = SUBLANES

    def conv_silu(src_ref, w_ref, b_ref, dst):
        xpad[0:pad, :] = jnp.zeros((pad, LANES), F32)
        xpad[pad:, :] = src_ref[...].astype(F32)
        y = b_ref[...] + jnp.zeros((seq, LANES), F32)
        for j in range(CONV_WIDTH):
            y = y + w_ref[j:j + 1, :] * xpad[pl.ds(pad - (CONV_WIDTH - 1) + j, seq), :]
        dst[...] = y * jax.nn.sigmoid(y)

    conv_silu(xs_ref, cwx_ref, cbx_ref, x_s)
    conv_silu(bm_ref, cwb_ref, cbb_ref, b_s)
    conv_silu(cm_ref, cwc_ref, cbc_ref, c_s)

    @pl.when(g == 0)
    def _():
        xr = dtr_ref[...] + dtb_ref[...]
        dt = jnp.maximum(xr, 0.0) + jnp.log1p(jnp.exp(-jnp.abs(xr)))
        for i, piece in enumerate(_split3(dt)):
            dt_s[i] = piece
        for i, piece in enumerate(_split3(dt * (-jnp.exp(alog_ref[...])))):
            da_s[i] = piece

    def select(v_s, i, sel):
        return jnp.dot(v_s[i], sel, preferred_element_type=F32)

    hrow = lax.broadcasted_iota(jnp.int32, (LANES, 2 * LANES), 0)
    hlane = lax.broadcasted_iota(jnp.int32, (LANES, 2 * LANES), 1)
    brow = lax.broadcasted_iota(jnp.int32, (LANES, LANES), 0)
    blane = lax.broadcasted_iota(jnp.int32, (LANES, LANES), 1)
    sel_b = (brow == 2 * g + (blane >= SSM_HEAD_DIM).astype(jnp.int32)).astype(BF16)
    sel_f = (hrow == 2 * g + (hlane >= LANES).astype(jnp.int32)).astype(BF16)
    dtb_s[...] = select(dt_s, 0, sel_b) + select(dt_s, 1, sel_b) + select(dt_s, 2, sel_b)
    for i in range(3):
        daf_s[i] = select(da_s, i, sel_f).astype(BF16)

    li = lax.broadcasted_iota(jnp.int32, (ck, ck), 0)
    si = lax.broadcasted_iota(jnp.int32, (ck, ck), 1)
    causal = si <= li
    eye = si == li
    tril_bf = causal.astype(BF16)
    head1 = lax.broadcasted_iota(jnp.int32, (ck, LANES), 1) >= SSM_HEAD_DIM
    head1_row = lax.broadcasted_iota(jnp.int32, (1, LANES), 1) >= SSM_HEAD_DIM

    def chunk(c, prev):
        r = pl.ds(pl.multiple_of(c * ck, ck), ck)
        xc = x_s[r, :]
        bc = b_s[r, :].astype(BF16)
        cc = c_s[r, :].astype(BF16)
        acs2 = (jnp.dot(tril_bf, daf_s[0, r, :], preferred_element_type=F32)
                + jnp.dot(tril_bf, daf_s[1, r, :], preferred_element_type=F32)
                + jnp.dot(tril_bf, daf_s[2, r, :], preferred_element_type=F32))
        acs_f = (acs2[:, :LANES], acs2[:, LANES:])
        acs_b = jnp.where(head1, acs_f[1], acs_f[0])
        last_b = jnp.where(head1_row, acs_f[1][ck - 1:ck, :], acs_f[0][ck - 1:ck, :])
        cb = lax.dot_general(cc, bc, (((1,), (1,)), ((), ())), preferred_element_type=F32)
        xdt = xc * dtb_s[r, :]
        scores = []
        for k in range(2):
            acs_row = jnp.sum(jnp.where(eye, acs_f[k], 0.0), axis=0, keepdims=True)
            decay = jnp.exp(jnp.where(causal, acs_f[k] - acs_row, -jnp.inf))
            scores.append((cb * decay).astype(BF16))
        xdt2 = jnp.concatenate([jnp.where(head1, 0.0, xdt).astype(BF16),
                                jnp.where(head1, xdt, 0.0).astype(BF16)], axis=0)
        y_diag = jnp.dot(jnp.concatenate(scores, axis=1), xdt2, preferred_element_type=F32)
        y_off = jnp.dot(cc, prev.astype(BF16), preferred_element_type=F32) * jnp.exp(acs_b)
        y = y_diag + y_off + dsk_ref[...] * xc
        zc = z_ref[r, :].astype(F32)
        y = y * (zc * jax.nn.sigmoid(zc))
        y = y * lax.rsqrt(jnp.mean(y * y, axis=-1, keepdims=True) + EPS) * nw_ref[...]
        o_ref[r, :] = y.astype(BF16)
        w = (xdt * jnp.exp(last_b - acs_b)).astype(BF16)
        st = lax.dot_general(bc, w, (((0,), (0,)), ((), ())), preferred_element_type=F32)
        return prev * jnp.exp(last_b) + st

    lax.fori_loop(0, seq // ck, chunk, jnp.zeros((SSM_STATE, LANES), F32), unroll=8)


def _ssd(proj, dt_raw, conv_w, conv_b, dtb, alog, dsk, nw, batch, seq):
    t = proj.shape[0]
    gs = SSM_GROUPS
    col = lambda off: (lambda b, g: (b, off + g))
    wcol = lambda off: (lambda b, g: (0, off + g))
    zq, xq, bq, cq = 3 * D_ATTN // LANES, 4 * D_ATTN // LANES, 5 * D_ATTN // LANES, 6 * D_ATTN // LANES
    const = lambda b, g: (0, 0)
    scr = pltpu.VMEM((seq, LANES), F32)
    return pl.pallas_call(
        _ssd_kernel,
        grid=(batch, gs),
        in_specs=[pl.BlockSpec((seq, LANES), col(xq)),
                  pl.BlockSpec((seq, LANES), col(bq)),
                  pl.BlockSpec((seq, LANES), col(cq)),
                  pl.BlockSpec((seq, LANES), col(zq)),
                  pl.BlockSpec((seq, LANES), lambda b, g: (b, 0)),
                  pl.BlockSpec((CONV_WIDTH, LANES), wcol(0)),
                  pl.BlockSpec((CONV_WIDTH, LANES), wcol(gs)),
                  pl.BlockSpec((CONV_WIDTH, LANES), wcol(2 * gs)),
                  pl.BlockSpec((1, LANES), wcol(0)),
                  pl.BlockSpec((1, LANES), wcol(gs)),
                  pl.BlockSpec((1, LANES), wcol(2 * gs)),
                  pl.BlockSpec((1, LANES), const),
                  pl.BlockSpec((1, LANES), const),
                  pl.BlockSpec((1, LANES), lambda b, g: (0, g)),
                  pl.BlockSpec((1, LANES), lambda b, g: (0, g))],
        out_specs=pl.BlockSpec((seq, LANES), lambda b, g: (b, g)),
        out_shape=jax.ShapeDtypeStruct((t, D_SSM), BF16),
        scratch_shapes=[pltpu.VMEM((seq + SUBLANES, LANES), F32), scr, scr, scr,
                        pltpu.VMEM((3, seq, LANES), BF16), pltpu.VMEM((3, seq, LANES), BF16), scr,
                        pltpu.VMEM((3, seq, 2 * LANES), BF16)],
        compiler_params=_cparams(("arbitrary", "arbitrary")),
        name="ssd",
    )(proj, proj, proj, proj, dt_raw, conv_w, conv_w, conv_w, conv_b, conv_b, conv_b,
      dtb, alog, dsk, nw)


def _out_router_kernel(ya_ref, ys_ref, wa_ref, ws_ref, x_ref, mod_ref, nw_ref, rw_ref, rb_ref,
                       x1_ref, h2_ref, eid_ref, ew_ref, rank_ref, cnt_ref, cnt_s):
    tm = x_ref.shape[0]
    mix = (jnp.dot(ya_ref[...], wa_ref[...], preferred_element_type=F32)
           + jnp.dot(ys_ref[...], ws_ref[...], preferred_element_type=F32))
    x1 = x_ref[...] + mod_ref[0, 2:3, :] * mix
    x1_ref[...] = x1
    h2 = _modulated_norm(x1, nw_ref[...], mod_ref[0, 3:4, :], mod_ref[0, 4:5, :])
    h2_ref[...] = _pack_bf16_pair(h2)

    h_hi = h2.astype(BF16)
    h_lo = (h2 - h_hi.astype(F32)).astype(BF16)
    rw = rw_ref[...]
    w_hi = rw.astype(BF16)
    w_lo = (rw - w_hi.astype(F32)).astype(BF16)
    nt = (((1,), (1,)), ((), ()))
    logits = (lax.dot_general(w_hi, h_hi, nt, preferred_element_type=F32)
              + lax.dot_general(w_hi, h_lo, nt, preferred_element_type=F32)
              + lax.dot_general(w_lo, h_hi, nt, preferred_element_type=F32)) + rb_ref[...]

    ng, ne = N_EXPERT_GROUPS, EXPERTS_PER_GROUP
    iota = lax.broadcasted_iota(jnp.int32, (ne, tm), 0)

    def argmax0(v):
        mx = jnp.max(v, axis=0, keepdims=True)
        idx = jnp.min(jnp.where(v == mx, iota, ne), axis=0, keepdims=True)
        return mx, idx

    gl = logits[N_EXPERTS:N_EXPERTS + ng, :]
    gmax, gidx = argmax0(gl)
    g_gate = 1.0 / jnp.sum(jnp.exp(gl - gmax), axis=0, keepdims=True)
    e_in = jnp.zeros((ne, tm), F32)
    for gi in range(ng):
        e_in = e_in + jnp.where(gidx == gi, logits[gi * ne:(gi + 1) * ne, :], 0.0)
    m1, i1 = argmax0(e_in)
    m2, i2 = argmax0(jnp.where(iota == i1, -jnp.inf, e_in))
    p2 = jnp.exp(m2 - m1)
    den = 1.0 + p2
    e0 = gidx * ne + i1
    e1 = gidx * ne + i2
    eid_ref[0, 0:1, :] = e0
    eid_ref[0, 1:2, :] = e1
    ew_ref[0:1, :] = (1.0 / den) * g_gate
    ew_ref[1:2, :] = (p2 / den) * g_gate

    @pl.when(pl.program_id(0) == 0)
    def _():
        cnt_s[...] = jnp.zeros(cnt_s.shape, F32)

    eiota = lax.broadcasted_iota(jnp.int32, (N_EXPERTS, tm), 0)
    oh0 = eiota == e0
    oh1 = eiota == e1
    both = jnp.logical_or(oh0, oh1)
    ti = lax.broadcasted_iota(jnp.int32, (tm, tm), 0)
    tj = lax.broadcasted_iota(jnp.int32, (tm, tm), 1)
    earlier = (ti < tj).astype(BF16)
    before = cnt_s[...] + jnp.dot(both.astype(BF16), earlier, preferred_element_type=F32)
    rank_ref[0, 0:1, :] = jnp.sum(jnp.where(oh0, before, 0.0), axis=0, keepdims=True).astype(jnp.int32)
    rank_ref[0, 1:2, :] = jnp.sum(jnp.where(oh1, before, 0.0), axis=0, keepdims=True).astype(jnp.int32)
    total = cnt_s[...] + jnp.sum(both.astype(F32), axis=1, keepdims=True)
    cnt_s[...] = total
    cnt_ref[...] = total.astype(jnp.int32)


def _out_router(ya, ys, wa, ws, x2, mod3, norm_w, rw, rb, seq):
    t, d = x2.shape
    tm = OUT_TM
    const = lambda i: (0, 0)
    return pl.pallas_call(
        _out_router_kernel,
        grid=(t // tm,),
        in_specs=[pl.BlockSpec((tm, D_ATTN), lambda i: (i, 0)),
                  pl.BlockSpec((tm, D_SSM), lambda i: (i, 0)),
                  pl.BlockSpec((D_ATTN, d), const, pipeline_mode=pl.Buffered(1)),
                  pl.BlockSpec((D_SSM, d), const, pipeline_mode=pl.Buffered(1)),
                  pl.BlockSpec((tm, d), lambda i: (i, 0)),
                  pl.BlockSpec((1, 6, d), lambda i: (i * tm // seq, 0, 0)),
                  pl.BlockSpec((1, d), const),
                  pl.BlockSpec((LANES, d), const, pipeline_mode=pl.Buffered(1)),
                  pl.BlockSpec((LANES, 1), const)],
        out_specs=[pl.BlockSpec((tm, d), lambda i: (i, 0)),
                   pl.BlockSpec((tm, d // 2), lambda i: (i, 0)),
                   pl.BlockSpec((1, 2, tm), lambda i: (i, 0, 0)),
                   pl.BlockSpec((2, tm), lambda i: (0, i)),
                   pl.BlockSpec((1, 2, tm), lambda i: (i, 0, 0)),
                   pl.BlockSpec((N_EXPERTS, 1), const)],
        out_shape=[jax.ShapeDtypeStruct((t, d), F32),
                   jax.ShapeDtypeStruct((t, d // 2), jnp.uint32),
                   jax.ShapeDtypeStruct((t // tm, 2, tm), jnp.int32),
                   jax.ShapeDtypeStruct((2, t), F32),
                   jax.ShapeDtypeStruct((t // tm, 2, tm), jnp.int32),
                   jax.ShapeDtypeStruct((N_EXPERTS, 1), jnp.int32)],
        scratch_shapes=[pltpu.VMEM((N_EXPERTS, 1), F32)],
        compiler_params=_cparams(("arbitrary",)),
        name="out_router",
    )(ya, ys, wa, ws, x2, mod3, norm_w, rw, rb)


def _row_of(pstart_ref, eid_ref, rank_ref, k, t):
    return pstart_ref[eid_ref[0, k, t]] + rank_ref[0, k, t]


def _scatter_kernel(pstart_ref, zstart_ref, eid_ref, rank_ref, h_ref, xd_hbm, sem, zbuf, zsem):
    tm = h_ref.shape[0]

    @pl.when(pl.program_id(0) == 0)
    def _():
        zbuf[...] = jnp.zeros(zbuf.shape, zbuf.dtype)

        def zero_copy(j):
            start = pl.multiple_of(jnp.maximum(zstart_ref[j], 0), MOE_ROWS)
            return pltpu.make_async_copy(zbuf, xd_hbm.at[pl.ds(start, MOE_ROWS)], zsem)

        def z_issue(j, carry):
            @pl.when(zstart_ref[j] >= 0)
            def _():
                zero_copy(j).start()
            return carry

        def z_wait(j, carry):
            @pl.when(zstart_ref[j] >= 0)
            def _():
                zero_copy(j).wait()
            return carry

        lax.fori_loop(0, zstart_ref.shape[0], z_issue, 0)
        lax.fori_loop(0, zstart_ref.shape[0], z_wait, 0)

    def issue(t, carry):
        for k in range(2):
            row = _row_of(pstart_ref, eid_ref, rank_ref, k, t)
            pltpu.make_async_copy(h_ref.at[pl.ds(t, 1)], xd_hbm.at[pl.ds(row, 1)], sem).start()
        return carry

    lax.fori_loop(0, tm, issue, 0, unroll=8)
    for _ in range(2):
        pltpu.make_async_copy(h_ref, xd_hbm.at[pl.ds(0, tm)], sem).wait()


def _scatter_rows(pstart, zstart, eid3, rank3, h2, n_rows):
    t, d = h2.shape
    n_tiles, _, tm = eid3.shape
    smem = lambda: pl.BlockSpec((1, 2, tm), lambda i, ps, zs: (i, 0, 0), memory_space=pltpu.SMEM)
    grid_spec = pltpu.PrefetchScalarGridSpec(
        num_scalar_prefetch=2,
        grid=(n_tiles,),
        in_specs=[smem(), smem(), pl.BlockSpec((tm, d), lambda i, ps, zs: (i, 0))],
        out_specs=pl.BlockSpec(memory_space=pl.ANY),
        scratch_shapes=[pltpu.SemaphoreType.DMA(()), pltpu.VMEM((MOE_ROWS, d), h2.dtype),
                        pltpu.SemaphoreType.DMA(())],
    )
    return pl.pallas_call(
        _scatter_kernel,
        grid_spec=grid_spec,
        out_shape=jax.ShapeDtypeStruct((n_rows, d), h2.dtype),
        compiler_params=_cparams(("arbitrary",)),
        name="scatter_rows",
    )(pstart, zstart, eid3, rank3, h2)


def _experts_kernel(be_ref, nv_ref, x_ref, wg_ref, wu_ref, wd_ref, o_ref, wg_s, wu_s, wd_s):
    i = pl.program_id(0)
    n_valid = nv_ref[i]
    changed = jnp.logical_or(i == 0, be_ref[i] != be_ref[jnp.maximum(i - 1, 0)])
    half = x_ref.shape[1]

    @pl.when(jnp.logical_and(changed, n_valid > 0))
    def _():
        wg_s[...] = wg_ref[0].astype(BF16)
        wu_s[...] = wu_ref[0].astype(BF16)
        wd_s[...] = wd_ref[0].astype(BF16)

    @pl.when(n_valid > 0)
    def _():
        x_lo, x_hi = _unpack_bf16_pair(x_ref[...])
        x_lo = x_lo.astype(BF16)
        x_hi = x_hi.astype(BF16)
        gt = (jnp.dot(x_lo, wg_s[0:half, :], preferred_element_type=F32)
              + jnp.dot(x_hi, wg_s[half:, :], preferred_element_type=F32))
        up = (jnp.dot(x_lo, wu_s[0:half, :], preferred_element_type=F32)
              + jnp.dot(x_hi, wu_s[half:, :], preferred_element_type=F32))
        hid = (gt * jax.nn.sigmoid(gt)) * up
        o_ref[...] = _pack_bf16_pair(jnp.dot(hid.astype(BF16), wd_s[...], preferred_element_type=F32))

    @pl.when(n_valid == 0)
    def _():
        o_ref[...] = jnp.zeros(o_ref.shape, o_ref.dtype)


def _experts(block_e, n_valid, xd, w_gate, w_up, w_down):
    rows = MOE_ROWS
    n_blocks = xd.shape[0] // rows
    half = xd.shape[1]
    d = 2 * half
    de = w_gate.shape[2]
    grid_spec = pltpu.PrefetchScalarGridSpec(
        num_scalar_prefetch=2,
        grid=(n_blocks,),
        in_specs=[pl.BlockSpec((rows, half), lambda i, be, nv: (i, 0)),
                  pl.BlockSpec((1, d, de), lambda i, be, nv: (be[i], 0, 0)),
                  pl.BlockSpec((1, d, de), lambda i, be, nv: (be[i], 0, 0)),
                  pl.BlockSpec((1, de, d), lambda i, be, nv: (be[i], 0, 0))],
        out_specs=pl.BlockSpec((rows, half), lambda i, be, nv: (i, 0)),
        scratch_shapes=[pltpu.VMEM((d, de), BF16),
                        pltpu.VMEM((d, de), BF16),
                        pltpu.VMEM((de, d), BF16)],
    )
    return pl.pallas_call(
        _experts_kernel,
        grid_spec=grid_spec,
        out_shape=jax.ShapeDtypeStruct((n_blocks * rows, half), jnp.uint32),
        compiler_params=_cparams(("arbitrary",)),
        name="experts",
    )(block_e, n_valid, xd, w_gate, w_up, w_down)


def _combine_kernel(pstart_ref, eid_ref, rank_ref, eidn_ref, rankn_ref, yd_hbm, x1_ref, mod_ref, ew_ref,
                    o_ref, ybuf, sems):
    i = pl.program_id(0)
    n = pl.num_programs(0)
    tm = x1_ref.shape[0]
    slot = i % 2

    def gather(e_ref, r_ref, s):
        def issue(t, carry):
            for k in range(2):
                row = _row_of(pstart_ref, e_ref, r_ref, k, t)
                pltpu.make_async_copy(yd_hbm.at[pl.ds(row, 1)], ybuf.at[s, pl.ds(k * tm + t, 1)],
                                      sems.at[s]).start()
            return carry
        lax.fori_loop(0, tm, issue, 0, unroll=8)

    @pl.when(i == 0)
    def _():
        gather(eid_ref, rank_ref, 0)

    @pl.when(i + 1 < n)
    def _():
        gather(eidn_ref, rankn_ref, 1 - slot)

    pltpu.make_async_copy(yd_hbm.at[pl.ds(0, 2 * tm)], ybuf.at[slot], sems.at[slot]).wait()
    half = ybuf.shape[2]
    y0_lo, y0_hi = _unpack_bf16_pair(ybuf[slot, 0:tm, :])
    y1_lo, y1_hi = _unpack_bf16_pair(ybuf[slot, tm:2 * tm, :])
    w0 = ew_ref[:, 0:1]
    w1 = ew_ref[:, 1:2]
    o_ref[:, 0:half] = x1_ref[:, 0:half] + mod_ref[0, 5:6, 0:half] * (w0 * y0_lo + w1 * y1_lo)
    o_ref[:, half:] = x1_ref[:, half:] + mod_ref[0, 5:6, half:] * (w0 * y0_hi + w1 * y1_hi)


def _combine(pstart, eid3, rank3, yd, x1, mod3, ew_t, seq):
    t, d = x1.shape
    n, _, tm = eid3.shape
    cur = lambda i, ps: (i, 0, 0)
    nxt = lambda i, ps: (jnp.minimum(i + 1, n - 1), 0, 0)
    smem = lambda im: pl.BlockSpec((1, 2, tm), im, memory_space=pltpu.SMEM)
    grid_spec = pltpu.PrefetchScalarGridSpec(
        num_scalar_prefetch=1,
        grid=(n,),
        in_specs=[smem(cur), smem(cur), smem(nxt), smem(nxt),
                  pl.BlockSpec(memory_space=pl.ANY),
                  pl.BlockSpec((tm, d), lambda i, ps: (i, 0)),
                  pl.BlockSpec((1, 6, d), lambda i, ps: (i * tm // seq, 0, 0)),
                  pl.BlockSpec((tm, 2), lambda i, ps: (i, 0))],
        out_specs=pl.BlockSpec((tm, d), lambda i, ps: (i, 0)),
        scratch_shapes=[pltpu.VMEM((2, 2 * tm, yd.shape[1]), yd.dtype), pltpu.SemaphoreType.DMA((2,))],
    )
    return pl.pallas_call(
        _combine_kernel,
        grid_spec=grid_spec,
        out_shape=jax.ShapeDtypeStruct((t, d), F32),
        compiler_params=_cparams(("arbitrary",)),
        name="combine",
    )(pstart, eid3, rank3, eid3, rank3, yd, x1, mod3, ew_t)


def _layout(counts, n_assign):
    rows = MOE_ROWS
    n_blocks = (n_assign + N_EXPERTS * (rows - 1)) // rows
    padded = (counts + rows - 1) // rows * rows
    pend = jnp.cumsum(padded)
    pstart = pend - padded
    bstart = jnp.arange(n_blocks, dtype=jnp.int32) * rows
    block_e = jnp.minimum(jnp.sum((pend[None, :] <= bstart[:, None]).astype(jnp.int32), axis=1),
                          N_EXPERTS - 1)
    n_valid = jnp.clip(pstart[block_e] + counts[block_e] - bstart, 0, rows)
    n_valid = jnp.where(bstart < pend[-1], n_valid, 0)
    min_used = n_assign // rows
    tail = pend[-1] + jnp.arange(n_blocks - min_used, dtype=jnp.int32) * rows
    zstart = jnp.concatenate([jnp.where(counts > 0, pend - rows, -1),
                              jnp.where(tail < n_blocks * rows, tail, -1)])
    return (pstart.astype(jnp.int32), zstart.astype(jnp.int32), block_e.astype(jnp.int32),
            n_valid.astype(jnp.int32), n_blocks * rows)


def kernel(x, c, positions, ada_w, ada_b, norm1_w, w_in, q_norm_w, k_norm_w, conv_w, conv_b,
           dt_bias, a_log, d_skip, ssm_norm_w, w_out, norm2_w, router_group_w, router_group_b,
           router_expert_w, router_expert_b, w_gate, w_up, w_down):
    batch, seq, d = x.shape
    t = batch * seq
    depth = ada_w.shape[0]
    nh_pad = LANES - N_SSM_HEADS
    inv_freq = ROPE_THETA ** (-(jnp.arange(0, ROT_DIM, 2, dtype=F32) / ROT_DIM))
    freq = jnp.concatenate([inv_freq, inv_freq, jnp.zeros((HEAD_DIM - ROT_DIM,), F32)]).reshape(1, HEAD_DIM)
    pos3 = positions.reshape(batch, seq, 1)
    xc = x.reshape(t, d)
    for l in range(depth):
        mod3 = _adaln(c, ada_w[l], ada_b[l]).reshape(batch, 6, d)
        w_main = w_in[l][:, :N_MAIN].astype(BF16)
        w_dt = jnp.pad(w_in[l][:, N_MAIN:], ((0, 0), (0, nh_pad))).astype(BF16)
        proj, dt_raw = _in_proj(xc, mod3, norm1_w[l].reshape(1, d), w_main, w_dt, seq)
        y_attn = _attention(proj, pos3, freq, q_norm_w[l].reshape(1, HEAD_DIM),
                            k_norm_w[l].reshape(1, HEAD_DIM), batch, seq)
        y_ssm = _ssd(proj, dt_raw, conv_w[l], conv_b[l].reshape(1, -1),
                     jnp.pad(dt_bias[l], (0, nh_pad)).reshape(1, LANES),
                     jnp.pad(a_log[l], (0, nh_pad)).reshape(1, LANES),
                     jnp.repeat(d_skip[l], SSM_HEAD_DIM).reshape(1, D_SSM),
                     ssm_norm_w[l].reshape(1, D_SSM), batch, seq)
        rw = jnp.concatenate([router_expert_w[l].T, router_group_w[l].T,
                              jnp.zeros((LANES - N_EXPERTS - N_EXPERT_GROUPS, d), F32)])
        rb = jnp.concatenate([router_expert_b[l], router_group_b[l],
                              jnp.zeros((LANES - N_EXPERTS - N_EXPERT_GROUPS,), F32)]).reshape(LANES, 1)
        wo = w_out[l].astype(BF16)
        x1, h2, eid3, ew, rank3, counts = _out_router(y_attn, y_ssm, wo[:D_ATTN], wo[D_ATTN:], xc, mod3,
                                                      norm2_w[l].reshape(1, d), rw, rb, seq)
        pstart, zstart, block_e, n_valid, n_rows = _layout(counts.reshape(N_EXPERTS), 2 * t)
        xd = _scatter_rows(pstart, zstart, eid3, rank3, h2, n_rows)
        yd = _experts(block_e, n_valid, xd, w_gate[l], w_up[l], w_down[l])
        xc = _combine(pstart, eid3, rank3, yd, x1, mod3, ew.T, seq)
    return xc.reshape(batch, seq, d)
```

```python
import jax
import jax.numpy as jnp
from jax import lax
from jax.experimental import pallas as pl
from jax.experimental.pallas import tpu as pltpu

F32 = jnp.float32
BF16 = jnp.bfloat16

D_MODEL = 2048
D_ATTN = 1024
D_SSM = 1024
HEAD_DIM = 128
N_ATTN_HEADS = 8
ROT_DIM = 32
ROPE_THETA = 500000.0
ATTN_BLOCK = 128
DILATIONS = (1, 4, 16)
SSM_HEAD_DIM = 64
N_SSM_HEADS = 16
SSM_GROUPS = 8
SSM_STATE = 128
CONV_WIDTH = 4
SSD_CHUNK = 128
N_MAIN = 3 * D_ATTN + D_SSM + D_SSM + 2 * SSM_GROUPS * SSM_STATE
N_EXPERT_GROUPS = 8
EXPERTS_PER_GROUP = 8
N_EXPERTS = 64
D_EXPERT = 512
EPS = 1e-6

LANES = 128
SUBLANES = 8
VMEM_LIMIT = 56 * 1024 * 1024

ADALN_TN = 1024
INPROJ_TM = 1024
INPROJ_TN = 1024
ATTN_GROUP = 5
OUT_TM = 512
MOE_ROWS = 256


def _cparams(sem):
    return pltpu.CompilerParams(dimension_semantics=sem, vmem_limit_bytes=VMEM_LIMIT)


def _pack_bf16_pair(x):
    k = x.shape[1] // 2
    lo = lax.bitcast_convert_type(x[:, :k].astype(BF16).astype(F32), jnp.uint32)
    hi = lax.bitcast_convert_type(x[:, k:].astype(BF16).astype(F32), jnp.uint32)
    return (lo >> 16) | (hi & jnp.uint32(0xFFFF0000))


def _unpack_bf16_pair(p):
    lo = lax.bitcast_convert_type(p << 16, F32)
    hi = lax.bitcast_convert_type(p & jnp.uint32(0xFFFF0000), F32)
    return lo, hi


def _adaln_kernel(c_ref, w_ref, b_ref, o_ref):
    c = c_ref[...]
    ca = (c * jax.nn.sigmoid(c)).astype(BF16)
    o_ref[...] = jnp.dot(ca, w_ref[...].astype(BF16), preferred_element_type=F32) + b_ref[...]


def _adaln(c, ada_w, ada_b):
    b, d = c.shape
    n = ada_w.shape[1]
    return pl.pallas_call(
        _adaln_kernel,
        grid=(n // ADALN_TN,),
        in_specs=[pl.BlockSpec((b, d), lambda j: (0, 0)),
                  pl.BlockSpec((d, ADALN_TN), lambda j: (0, j)),
                  pl.BlockSpec((1, ADALN_TN), lambda j: (0, j))],
        out_specs=pl.BlockSpec((b, ADALN_TN), lambda j: (0, j)),
        out_shape=jax.ShapeDtypeStruct((b, n), F32),
        compiler_params=_cparams(("arbitrary",)),
        name="adaln",
    )(c, ada_w, ada_b.reshape(1, n))


def _modulated_norm(x, nw, shift, scale):
    y = x * lax.rsqrt(jnp.mean(x * x, axis=-1, keepdims=True) + EPS) * nw
    return y * (1.0 + scale) + shift


def _inproj_kernel(x_ref, mod_ref, nw_ref, w_ref, wdt_ref, o_ref, dt_ref, h_scr):
    @pl.when(pl.program_id(1) == 0)
    def _():
        h = _modulated_norm(x_ref[...], nw_ref[...], mod_ref[0, 0:1, :], mod_ref[0, 1:2, :])
        h_scr[...] = h.astype(BF16)
        dt_ref[...] = jnp.dot(h_scr[...], wdt_ref[...], preferred_element_type=F32)

    o_ref[...] = jnp.dot(h_scr[...], w_ref[...], preferred_element_type=F32).astype(BF16)


def _in_proj(x2, mod3, norm_w, w_main, w_dt, seq):
    t, d = x2.shape
    tm, tn = INPROJ_TM, INPROJ_TN
    return pl.pallas_call(
        _inproj_kernel,
        grid=(t // tm, N_MAIN // tn),
        in_specs=[pl.BlockSpec((tm, d), lambda i, j: (i, 0)),
                  pl.BlockSpec((1, 6, d), lambda i, j: (i * tm // seq, 0, 0)),
                  pl.BlockSpec((1, d), lambda i, j: (0, 0)),
                  pl.BlockSpec((d, tn), lambda i, j: (0, j)),
                  pl.BlockSpec((d, LANES), lambda i, j: (0, 0))],
        out_specs=[pl.BlockSpec((tm, tn), lambda i, j: (i, j)),
                   pl.BlockSpec((tm, LANES), lambda i, j: (i, 0))],
        out_shape=[jax.ShapeDtypeStruct((t, N_MAIN), BF16),
                   jax.ShapeDtypeStruct((t, LANES), F32)],
        scratch_shapes=[pltpu.VMEM((tm, d), BF16)],
        compiler_params=_cparams(("parallel", "arbitrary")),
        name="in_proj",
    )(x2, mod3, norm_w, w_main, w_dt)


def _attn_kernel(q_ref, k_ref, v_ref, pos_ref, freq_ref, qw_ref, kw_ref, o_ref,
                 cos_s, sa_s, sb_s, qs, ks, vs, acc_s, m_s, l_s, tmp):
    seq = q_ref.shape[0]
    blk = ATTN_BLOCK
    half = ROT_DIM // 2

    @pl.when(pl.program_id(1) == 0)
    def _():
        ang = pos_ref[0].astype(F32) * freq_ref[...]
        lane = lax.broadcasted_iota(jnp.int32, (seq, LANES), 1)
        sn = jnp.sin(ang)
        cos_s[...] = jnp.cos(ang)
        sa_s[...] = jnp.where(lane < half, -sn, 0.0)
        sb_s[...] = jnp.where((lane >= half) & (lane < ROT_DIM), sn, 0.0)

    def norm_rope(t_ref, w_ref):
        t = t_ref[...].astype(F32)
        y = t * lax.rsqrt(jnp.mean(t * t, axis=-1, keepdims=True) + EPS) * w_ref[...]
        return (y * cos_s[...] + pltpu.roll(y, LANES - half, 1) * sa_s[...]
                + pltpu.roll(y, half, 1) * sb_s[...])

    qseg = seq // 4
    slab = blk // 4

    def to_quarter(val, dst):
        tmp[...] = val
        for r in range(4):
            dst[r * qseg:(r + 1) * qseg, :] = tmp[pl.ds(r, qseg, stride=4), :]

    to_quarter(norm_rope(q_ref, qw_ref), qs)
    to_quarter(norm_rope(k_ref, kw_ref), ks)
    to_quarter(v_ref[...].astype(F32), vs)

    qi = lax.broadcasted_iota(jnp.int32, (blk, blk), 0)
    kj = lax.broadcasted_iota(jnp.int32, (blk, blk), 1)
    pq = 4 * (qi % slab) + qi // slab
    pk = 4 * (kj % slab) + kj // slab
    masks = {
        "seq": (kj <= qi, jnp.concatenate([kj >= qi, kj <= qi], axis=1)),
        "slab": (pk <= pq, jnp.concatenate([pk >= pq, pk <= pq], axis=1)),
    }
    ones_blk = jnp.ones((blk, LANES), BF16)
    scale = HEAD_DIM ** -0.5
    nt = (((1,), (1,)), ((), ()))

    def load(ref, kind, start):
        if kind == "c":
            return ref[pl.ds(start, blk), :]
        if kind == "s":
            return ref[pl.ds(start, blk, stride=4), :]
        return jnp.concatenate([ref[pl.ds(start + r * qseg, slab), :] for r in range(4)], axis=0)

    def store(ref, kind, start, val):
        if kind == "c":
            ref[pl.ds(start, blk), :] = val
        elif kind == "s":
            ref[pl.ds(start, blk, stride=4), :] = val
        else:
            for r in range(4):
                ref[pl.ds(start + r * qseg, slab), :] = val[r * slab:(r + 1) * slab]

    def group(kind, starts, with_prev, first_pattern):
        step = slab if kind == "q" else blk
        qb = [load(qs, kind, s).astype(BF16) for s in starts]
        kb = [load(ks, kind, s).astype(BF16) for s in starts]
        vb = [jnp.concatenate([load(vs, kind, s).astype(BF16), ones_blk], axis=1) for s in starts]
        if with_prev:
            kb = [jnp.concatenate([load(ks, kind, s - step).astype(BF16), k], axis=0)
                  for s, k in zip(starts, kb)]
            vb = [jnp.concatenate([jnp.concatenate([load(vs, kind, s - step).astype(BF16), ones_blk], axis=1),
                                   v], axis=0) for s, v in zip(starts, vb)]
        mask = masks["slab" if kind == "q" else "seq"][1 if with_prev else 0]
        sc = [jnp.where(mask, lax.dot_general(q, k, nt, preferred_element_type=F32) * scale, -jnp.inf)
              for q, k in zip(qb, kb)]
        if with_prev:
            mx = [jnp.max(jnp.maximum(x[:, :blk], x[:, blk:]), axis=-1, keepdims=True) for x in sc]
        else:
            mx = [jnp.max(x, axis=-1, keepdims=True) for x in sc]
        pr = [jnp.exp(x - m).astype(BF16) for x, m in zip(sc, mx)]
        ov = [jnp.dot(p, v, preferred_element_type=F32) for p, v in zip(pr, vb)]
        mb = [jnp.broadcast_to(m, (blk, LANES)) for m in mx]
        if first_pattern:
            new = [(m, o[:, LANES:], o[:, :LANES]) for m, o in zip(mb, ov)]
        else:
            old = [(load(m_s, kind, s), load(l_s, kind, s), load(acc_s, kind, s)) for s in starts]
            new = []
            for (m_old, l_old, acc_old), m, o in zip(old, mb, ov):
                m_new = jnp.maximum(m_old, m)
                a_old = jnp.exp(m_old - m_new)
                a_new = jnp.exp(m - m_new)
                new.append((m_new, a_old * l_old + a_new * o[:, LANES:],
                            a_old * acc_old + a_new * o[:, :LANES]))
        for s, (m_new, l_new, acc_new) in zip(starts, new):
            store(m_s, kind, s, m_new)
            store(l_s, kind, s, l_new)
            store(acc_s, kind, s, acc_new)

    def loop(n_iter, kind, starts_of, with_prev, first_pattern):
        def body(i, carry):
            group(kind, starts_of(i), with_prev, first_pattern)
            return carry
        lax.fori_loop(0, n_iter, body, 0)

    nb4 = qseg // blk
    group("c", [r * qseg for r in range(4)], False, True)
    loop(2, "c", lambda i: [(2 * i + j) * qseg + n * blk for j in range(2) for n in range(1, nb4)], True, True)
    loop(4, "s", lambda r: [r * qseg + a for a in range(4)], False, False)
    group("q", [0], False, False)
    loop((seq // blk - 1) // ATTN_GROUP, "q",
         lambda i: [(1 + ATTN_GROUP * i + j) * slab for j in range(ATTN_GROUP)], True, False)

    tmp[...] = acc_s[...] / l_s[...]
    for r in range(4):
        qs[pl.ds(r, qseg, stride=4), :] = tmp[r * qseg:(r + 1) * qseg, :]
    o_ref[...] = qs[...].astype(BF16)


def _attention(proj, pos3, freq, qw, kw, batch, seq):
    t = proj.shape[0]
    hd = HEAD_DIM
    nh = N_ATTN_HEADS
    assert seq % (ATTN_BLOCK * max(DILATIONS)) == 0 and (seq // ATTN_BLOCK - 1) % ATTN_GROUP == 0
    scr = pltpu.VMEM((seq, hd), F32)
    return pl.pallas_call(
        _attn_kernel,
        grid=(batch, nh),
        in_specs=[pl.BlockSpec((seq, hd), lambda b, h: (b, h)),
                  pl.BlockSpec((seq, hd), lambda b, h: (b, nh + h)),
                  pl.BlockSpec((seq, hd), lambda b, h: (b, 2 * nh + h)),
                  pl.BlockSpec((1, seq, 1), lambda b, h: (b, 0, 0)),
                  pl.BlockSpec((1, hd), lambda b, h: (0, 0)),
                  pl.BlockSpec((1, hd), lambda b, h: (0, 0)),
                  pl.BlockSpec((1, hd), lambda b, h: (0, 0))],
        out_specs=pl.BlockSpec((seq, hd), lambda b, h: (b, h)),
        out_shape=jax.ShapeDtypeStruct((t, D_ATTN), BF16),
        scratch_shapes=[scr] * 10,
        compiler_params=_cparams(("parallel", "arbitrary")),
        name="attention",
    )(proj, proj, proj, pos3, freq, qw, kw)


def _split3(v):
    hi = v.astype(BF16)
    r1 = v - hi.astype(F32)
    mid = r1.astype(BF16)
    lo = (r1 - mid.astype(F32)).astype(BF16)
    return hi, mid, lo


def _ssd_kernel(xs_ref, bm_ref, cm_ref, z_ref, dtr_ref, cwx_ref, cwb_ref, cwc_ref,
                cbx_ref, cbb_ref, cbc_ref, dtb_ref, alog_ref, dsk_ref, nw_ref, o_ref,
                xpad, x_s, b_s, c_s, dt_s, da_s, dtb_s, daf_s):
    seq = xs_ref.shape[0]
    ck = SSD_CHUNK
    g = pl.program_id(1)
    pad = SUBLANES

    def conv_silu(src_ref, w_ref, b_ref, dst):
        xpad[0:pad, :] = jnp.zeros((pad, LANES), F32)
        xpad[pad:, :] = src_ref[...].astype(F32)
        y = b_ref[...] + jnp.zeros((seq, LANES), F32)
        for j in range(CONV_WIDTH):
            y = y + w_ref[j:j + 1, :] * xpad[pl.ds(pad - (CONV_WIDTH - 1) + j, seq), :]
        dst[...] = y * jax.nn.sigmoid(y)

    conv_silu(xs_ref, cwx_ref, cbx_ref, x_s)
    conv_silu(bm_ref, cwb_ref, cbb_ref, b_s)
    conv_silu(cm_ref, cwc_ref, cbc_ref, c_s)

    @pl.when(g == 0)
    def _():
        xr = dtr_ref[...] + dtb_ref[...]
        dt = jnp.maximum(xr, 0.0) + jnp.log1p(jnp.exp(-jnp.abs(xr)))
        for i, piece in enumerate(_split3(dt)):
            dt_s[i] = piece
        for i, piece in enumerate(_split3(dt * (-jnp.exp(alog_ref[...])))):
            da_s[i] = piece

    def select(v_s, i, sel):
        return jnp.dot(v_s[i], sel, preferred_element_type=F32)

    hrow = lax.broadcasted_iota(jnp.int32, (LANES, 2 * LANES), 0)
    hlane = lax.broadcasted_iota(jnp.int32, (LANES, 2 * LANES), 1)
    brow = lax.broadcasted_iota(jnp.int32, (LANES, LANES), 0)
    blane = lax.broadcasted_iota(jnp.int32, (LANES, LANES), 1)
    sel_b = (brow == 2 * g + (blane >= SSM_HEAD_DIM).astype(jnp.int32)).astype(BF16)
    sel_f = (hrow == 2 * g + (hlane >= LANES).astype(jnp.int32)).astype(BF16)
    dtb_s[...] = select(dt_s, 0, sel_b) + select(dt_s, 1, sel_b) + select(dt_s, 2, sel_b)
    for i in range(3):
        daf_s[i] = select(da_s, i, sel_f).astype(BF16)

    li = lax.broadcasted_iota(jnp.int32, (ck, ck), 0)
    si = lax.broadcasted_iota(jnp.int32, (ck, ck), 1)
    causal = si <= li
    eye = si == li
    tril_bf = causal.astype(BF16)
    head1 = lax.broadcasted_iota(jnp.int32, (ck, LANES), 1) >= SSM_HEAD_DIM
    head1_row = lax.broadcasted_iota(jnp.int32, (1, LANES), 1) >= SSM_HEAD_DIM

    def chunk(c, prev):
        r = pl.ds(pl.multiple_of(c * ck, ck), ck)
        xc = x_s[r, :]
        bc = b_s[r, :].astype(BF16)
        cc = c_s[r, :].astype(BF16)
        acs2 = (jnp.dot(tril_bf, daf_s[0, r, :], preferred_element_type=F32)
                + jnp.dot(tril_bf, daf_s[1, r, :], preferred_element_type=F32)
                + jnp.dot(tril_bf, daf_s[2, r, :], preferred_element_type=F32))
        acs_f = (acs2[:, :LANES], acs2[:, LANES:])
        acs_b = jnp.where(head1, acs_f[1], acs_f[0])
        last_b = jnp.where(head1_row, acs_f[1][ck - 1:ck, :], acs_f[0][ck - 1:ck, :])
        cb = lax.dot_general(cc, bc, (((1,), (1,)), ((), ())), preferred_element_type=F32)
        xdt = xc * dtb_s[r, :]
        scores = []
        for k in range(2):
            acs_row = jnp.sum(jnp.where(eye, acs_f[k], 0.0), axis=0, keepdims=True)
            decay = jnp.exp(jnp.where(causal, acs_f[k] - acs_row, -jnp.inf))
            scores.append((cb * decay).astype(BF16))
        xdt2 = jnp.concatenate([jnp.where(head1, 0.0, xdt).astype(BF16),
                                jnp.where(head1, xdt, 0.0).astype(BF16)], axis=0)
        y_diag = jnp.dot(jnp.concatenate(scores, axis=1), xdt2, preferred_element_type=F32)
        y_off = jnp.dot(cc, prev.astype(BF16), preferred_element_type=F32) * jnp.exp(acs_b)
        y = y_diag + y_off + dsk_ref[...] * xc
        zc = z_ref[r, :].astype(F32)
        y = y * (zc * jax.nn.sigmoid(zc))
        y = y * lax.rsqrt(jnp.mean(y * y, axis=-1, keepdims=True) + EPS) * nw_ref[...]
        o_ref[r, :] = y.astype(BF16)
        w = (xdt * jnp.exp(last_b - acs_b)).astype(BF16)
        st = lax.dot_general(bc, w, (((0,), (0,)), ((), ())), preferred_element_type=F32)
        return prev * jnp.exp(last_b) + st

    lax.fori_loop(0, seq // ck, chunk, jnp.zeros((SSM_STATE, LANES), F32), unroll=8)


def _ssd(proj, dt_raw, conv_w, conv_b, dtb, alog, dsk, nw, batch, seq):
    t = proj.shape[0]
    gs = SSM_GROUPS
    col = lambda off: (lambda b, g: (b, off + g))
    wcol = lambda off: (lambda b, g: (0, off + g))
    zq, xq, bq, cq = 3 * D_ATTN // LANES, 4 * D_ATTN // LANES, 5 * D_ATTN // LANES, 6 * D_ATTN // LANES
    const = lambda b, g: (0, 0)
    scr = pltpu.VMEM((seq, LANES), F32)
    return pl.pallas_call(
        _ssd_kernel,
        grid=(batch, gs),
        in_specs=[pl.BlockSpec((seq, LANES), col(xq)),
                  pl.BlockSpec((seq, LANES), col(bq)),
                  pl.BlockSpec((seq, LANES), col(cq)),
                  pl.BlockSpec((seq, LANES), col(zq)),
                  pl.BlockSpec((seq, LANES), lambda b, g: (b, 0)),
                  pl.BlockSpec((CONV_WIDTH, LANES), wcol(0)),
                  pl.BlockSpec((CONV_WIDTH, LANES), wcol(gs)),
                  pl.BlockSpec((CONV_WIDTH, LANES), wcol(2 * gs)),
                  pl.BlockSpec((1, LANES), wcol(0)),
                  pl.BlockSpec((1, LANES), wcol(gs)),
                  pl.BlockSpec((1, LANES), wcol(2 * gs)),
                  pl.BlockSpec((1, LANES), const),
                  pl.BlockSpec((1, LANES), const),
                  pl.BlockSpec((1, LANES), lambda b, g: (0, g)),
                  pl.BlockSpec((1, LANES), lambda b, g: (0, g))],
        out_specs=pl.BlockSpec((seq, LANES), lambda b, g: (b, g)),
        out_shape=jax.ShapeDtypeStruct((t, D_SSM), BF16),
        scratch_shapes=[pltpu.VMEM((seq + SUBLANES, LANES), F32), scr, scr, scr,
                        pltpu.VMEM((3, seq, LANES), BF16), pltpu.VMEM((3, seq, LANES), BF16), scr,
                        pltpu.VMEM((3, seq, 2 * LANES), BF16)],
        compiler_params=_cparams(("arbitrary", "arbitrary")),
        name="ssd",
    )(proj, proj, proj, proj, dt_raw, conv_w, conv_w, conv_w, conv_b, conv_b, conv_b,
      dtb, alog, dsk, nw)


def _out_router_kernel(ya_ref, ys_ref, wa_ref, ws_ref, x_ref, mod_ref, nw_ref, rw_ref, rb_ref,
                       x1_ref, h2_ref, eid_ref, ew_ref, rank_ref, cnt_ref, cnt_s):
    tm = x_ref.shape[0]
    mix = (jnp.dot(ya_ref[...], wa_ref[...], preferred_element_type=F32)
           + jnp.dot(ys_ref[...], ws_ref[...], preferred_element_type=F32))
    x1 = x_ref[...] + mod_ref[0, 2:3, :] * mix
    x1_ref[...] = x1
    h2 = _modulated_norm(x1, nw_ref[...], mod_ref[0, 3:4, :], mod_ref[0, 4:5, :])
    h2_ref[...] = _pack_bf16_pair(h2)

    h_hi = h2.astype(BF16)
    h_lo = (h2 - h_hi.astype(F32)).astype(BF16)
    rw = rw_ref[...]
    w_hi = rw.astype(BF16)
    w_lo = (rw - w_hi.astype(F32)).astype(BF16)
    nt = (((1,), (1,)), ((), ()))
    logits = (lax.dot_general(w_hi, h_hi, nt, preferred_element_type=F32)
              + lax.dot_general(w_hi, h_lo, nt, preferred_element_type=F32)
              + lax.dot_general(w_lo, h_hi, nt, preferred_element_type=F32)) + rb_ref[...]

    ng, ne = N_EXPERT_GROUPS, EXPERTS_PER_GROUP
    iota = lax.broadcasted_iota(jnp.int32, (ne, tm), 0)

    def argmax0(v):
        mx = jnp.max(v, axis=0, keepdims=True)
        idx = jnp.min(jnp.where(v == mx, iota, ne), axis=0, keepdims=True)
        return mx, idx

    gl = logits[N_EXPERTS:N_EXPERTS + ng, :]
    gmax, gidx = argmax0(gl)
    g_gate = 1.0 / jnp.sum(jnp.exp(gl - gmax), axis=0, keepdims=True)
    e_in = jnp.zeros((ne, tm), F32)
    for gi in range(ng):
        e_in = e_in + jnp.where(gidx == gi, logits[gi * ne:(gi + 1) * ne, :], 0.0)
    m1, i1 = argmax0(e_in)
    m2, i2 = argmax0(jnp.where(iota == i1, -jnp.inf, e_in))
    p2 = jnp.exp(m2 - m1)
    den = 1.0 + p2
    e0 = gidx * ne + i1
    e1 = gidx * ne + i2
    eid_ref[0, 0:1, :] = e0
    eid_ref[0, 1:2, :] = e1
    ew_ref[0:1, :] = (1.0 / den) * g_gate
    ew_ref[1:2, :] = (p2 / den) * g_gate

    @pl.when(pl.program_id(0) == 0)
    def _():
        cnt_s[...] = jnp.zeros(cnt_s.shape, F32)

    eiota = lax.broadcasted_iota(jnp.int32, (N_EXPERTS, tm), 0)
    oh0 = eiota == e0
    oh1 = eiota == e1
    both = jnp.logical_or(oh0, oh1)
    ti = lax.broadcasted_iota(jnp.int32, (tm, tm), 0)
    tj = lax.broadcasted_iota(jnp.int32, (tm, tm), 1)
    earlier = (ti < tj).astype(BF16)
    before = cnt_s[...] + jnp.dot(both.astype(BF16), earlier, preferred_element_type=F32)
    rank_ref[0, 0:1, :] = jnp.sum(jnp.where(oh0, before, 0.0), axis=0, keepdims=True).astype(jnp.int32)
    rank_ref[0, 1:2, :] = jnp.sum(jnp.where(oh1, before, 0.0), axis=0, keepdims=True).astype(jnp.int32)
    total = cnt_s[...] + jnp.sum(both.astype(F32), axis=1, keepdims=True)
    cnt_s[...] = total
    cnt_ref[...] = total.astype(jnp.int32)


def _out_router(ya, ys, wa, ws, x2, mod3, norm_w, rw, rb, seq):
    t, d = x2.shape
    tm = OUT_TM
    const = lambda i: (0, 0)
    return pl.pallas_call(
        _out_router_kernel,
        grid=(t // tm,),
        in_specs=[pl.BlockSpec((tm, D_ATTN), lambda i: (i, 0)),
                  pl.BlockSpec((tm, D_SSM), lambda i: (i, 0)),
                  pl.BlockSpec((D_ATTN, d), const, pipeline_mode=pl.Buffered(1)),
                  pl.BlockSpec((D_SSM, d), const, pipeline_mode=pl.Buffered(1)),
                  pl.BlockSpec((tm, d), lambda i: (i, 0)),
                  pl.BlockSpec((1, 6, d), lambda i: (i * tm // seq, 0, 0)),
                  pl.BlockSpec((1, d), const),
                  pl.BlockSpec((LANES, d), const, pipeline_mode=pl.Buffered(1)),
                  pl.BlockSpec((LANES, 1), const)],
        out_specs=[pl.BlockSpec((tm, d), lambda i: (i, 0)),
                   pl.BlockSpec((tm, d // 2), lambda i: (i, 0)),
                   pl.BlockSpec((1, 2, tm), lambda i: (i, 0, 0)),
                   pl.BlockSpec((2, tm), lambda i: (0, i)),
                   pl.BlockSpec((1, 2, tm), lambda i: (i, 0, 0)),
                   pl.BlockSpec((N_EXPERTS, 1), const)],
        out_shape=[jax.ShapeDtypeStruct((t, d), F32),
                   jax.ShapeDtypeStruct((t, d // 2), jnp.uint32),
                   jax.ShapeDtypeStruct((t // tm, 2, tm), jnp.int32),
                   jax.ShapeDtypeStruct((2, t), F32),
                   jax.ShapeDtypeStruct((t // tm, 2, tm), jnp.int32),
                   jax.ShapeDtypeStruct((N_EXPERTS, 1), jnp.int32)],
        scratch_shapes=[pltpu.VMEM((N_EXPERTS, 1), F32)],
        compiler_params=_cparams(("arbitrary",)),
        name="out_router",
    )(ya, ys, wa, ws, x2, mod3, norm_w, rw, rb)


def _for_each_assignment(tm, body):
    def group(gi, carry):
        for jj in range(SUBLANES):
            for k in range(2):
                body(k, gi, jj, k * tm + gi * SUBLANES + jj)
        return carry
    lax.fori_loop(0, tm // SUBLANES, group, 0)


def _rows3(a):
    return a.reshape(a.shape[0] // SUBLANES, SUBLANES, a.shape[1])


def _scatter_kernel(zstart_ref, dest_ref, h_ref, xd_hbm, sem, zbuf, zsem):
    tm = h_ref.shape[0] * SUBLANES

    @pl.when(pl.program_id(0) == 0)
    def _():
        zbuf[...] = jnp.zeros(zbuf.shape, zbuf.dtype)

        def zero_copy(j):
            start = pl.multiple_of(jnp.maximum(zstart_ref[j], 0), MOE_ROWS)
            return pltpu.make_async_copy(zbuf, xd_hbm.at[pl.ds(start, MOE_ROWS)], zsem)

        def z_issue(j, carry):
            @pl.when(zstart_ref[j] >= 0)
            def _():
                zero_copy(j).start()
            return carry

        def z_wait(j, carry):
            @pl.when(zstart_ref[j] >= 0)
            def _():
                zero_copy(j).wait()
            return carry

        lax.fori_loop(0, zstart_ref.shape[0], z_issue, 0)
        lax.fori_loop(0, zstart_ref.shape[0], z_wait, 0)

    def issue(k, gi, jj, j):
        pltpu.make_async_copy(h_ref.at[gi, pl.ds(jj, 1)], xd_hbm.at[pl.ds(dest_ref[0, 0, j], 1)], sem).start()

    _for_each_assignment(tm, issue)
    for _ in range(2 * SUBLANES):
        pltpu.make_async_copy(h_ref.at[:, 0], xd_hbm.at[pl.ds(0, tm // SUBLANES)], sem).wait()


def _scatter_rows(zstart, dest3, h2, n_rows):
    t, d = h2.shape
    n_tiles = dest3.shape[0]
    tm = dest3.shape[2] // 2
    grid_spec = pltpu.PrefetchScalarGridSpec(
        num_scalar_prefetch=1,
        grid=(n_tiles,),
        in_specs=[pl.BlockSpec((1, 1, 2 * tm), lambda i, zs: (i, 0, 0), memory_space=pltpu.SMEM),
                  pl.BlockSpec((tm // SUBLANES, SUBLANES, d), lambda i, zs: (i, 0, 0))],
        out_specs=pl.BlockSpec(memory_space=pl.ANY),
        scratch_shapes=[pltpu.SemaphoreType.DMA(()), pltpu.VMEM((MOE_ROWS, d), h2.dtype),
                        pltpu.SemaphoreType.DMA(())],
    )
    return pl.pallas_call(
        _scatter_kernel,
        grid_spec=grid_spec,
        out_shape=jax.ShapeDtypeStruct((n_rows, d), h2.dtype),
        compiler_params=_cparams(("arbitrary",)),
        name="scatter_rows",
    )(zstart, dest3, _rows3(h2))


def _experts_kernel(be_ref, nv_ref, x_ref, wg_ref, wu_ref, wd_ref, o_ref, wg_s, wu_s, wd_s):
    i = pl.program_id(0)
    n_valid = nv_ref[i]
    changed = jnp.logical_or(i == 0, be_ref[i] != be_ref[jnp.maximum(i - 1, 0)])
    half = x_ref.shape[1]

    @pl.when(jnp.logical_and(changed, n_valid > 0))
    def _():
        wg_s[...] = wg_ref[0].astype(BF16)
        wu_s[...] = wu_ref[0].astype(BF16)
        wd_s[...] = wd_ref[0].astype(BF16)

    @pl.when(n_valid > 0)
    def _():
        x_lo, x_hi = _unpack_bf16_pair(x_ref[...])
        x_lo = x_lo.astype(BF16)
        x_hi = x_hi.astype(BF16)
        gt = (jnp.dot(x_lo, wg_s[0:half, :], preferred_element_type=F32)
              + jnp.dot(x_hi, wg_s[half:, :], preferred_element_type=F32))
        up = (jnp.dot(x_lo, wu_s[0:half, :], preferred_element_type=F32)
              + jnp.dot(x_hi, wu_s[half:, :], preferred_element_type=F32))
        hid = (gt * jax.nn.sigmoid(gt)) * up
        o_ref[...] = _pack_bf16_pair(jnp.dot(hid.astype(BF16), wd_s[...], preferred_element_type=F32))

    @pl.when(n_valid == 0)
    def _():
        o_ref[...] = jnp.zeros(o_ref.shape, o_ref.dtype)


def _experts(block_e, n_valid, xd, w_gate, w_up, w_down):
    rows = MOE_ROWS
    n_blocks = xd.shape[0] // rows
    half = xd.shape[1]
    d = 2 * half
    de = w_gate.shape[2]
    grid_spec = pltpu.PrefetchScalarGridSpec(
        num_scalar_prefetch=2,
        grid=(n_blocks,),
        in_specs=[pl.BlockSpec((rows, half), lambda i, be, nv: (i, 0)),
                  pl.BlockSpec((1, d, de), lambda i, be, nv: (be[i], 0, 0)),
                  pl.BlockSpec((1, d, de), lambda i, be, nv: (be[i], 0, 0)),
                  pl.BlockSpec((1, de, d), lambda i, be, nv: (be[i], 0, 0))],
        out_specs=pl.BlockSpec((rows, half), lambda i, be, nv: (i, 0)),
        scratch_shapes=[pltpu.VMEM((d, de), BF16),
                        pltpu.VMEM((d, de), BF16),
                        pltpu.VMEM((de, d), BF16)],
    )
    return pl.pallas_call(
        _experts_kernel,
        grid_spec=grid_spec,
        out_shape=jax.ShapeDtypeStruct((n_blocks * rows, half), jnp.uint32),
        compiler_params=_cparams(("arbitrary",)),
        name="experts",
    )(block_e, n_valid, xd, w_gate, w_up, w_down)


def _combine_kernel(dest_ref, destn_ref, yd_hbm, x1_ref, mod_ref, ew_ref, o_ref, ybuf, sems):
    i = pl.program_id(0)
    n = pl.num_programs(0)
    tm = x1_ref.shape[0]
    tg = tm // SUBLANES
    slot = i % 2

    def gather(d_ref, s):
        def issue(k, gi, jj, j):
            pltpu.make_async_copy(yd_hbm.at[pl.ds(d_ref[0, 0, j], 1)],
                                  ybuf.at[s, k * tg + gi, pl.ds(jj, 1)], sems.at[s]).start()
        _for_each_assignment(tm, issue)

    @pl.when(i == 0)
    def _():
        gather(dest_ref, 0)

    @pl.when(i + 1 < n)
    def _():
        gather(destn_ref, 1 - slot)

    for _ in range(SUBLANES):
        pltpu.make_async_copy(yd_hbm.at[pl.ds(0, 2 * tg)], ybuf.at[slot, :, 0], sems.at[slot]).wait()
    half = ybuf.shape[3]
    y0_lo, y0_hi = _unpack_bf16_pair(ybuf[slot, 0:tg].reshape(tm, half))
    y1_lo, y1_hi = _unpack_bf16_pair(ybuf[slot, tg:2 * tg].reshape(tm, half))
    w0 = ew_ref[:, 0:1]
    w1 = ew_ref[:, 1:2]
    o_ref[:, 0:half] = x1_ref[:, 0:half] + mod_ref[0, 5:6, 0:half] * (w0 * y0_lo + w1 * y1_lo)
    o_ref[:, half:] = x1_ref[:, half:] + mod_ref[0, 5:6, half:] * (w0 * y0_hi + w1 * y1_hi)


def _combine(dest3, yd, x1, mod3, ew_t, seq):
    t, d = x1.shape
    n = dest3.shape[0]
    tm = dest3.shape[2] // 2
    smem = lambda im: pl.BlockSpec((1, 1, 2 * tm), im, memory_space=pltpu.SMEM)
    return pl.pallas_call(
        _combine_kernel,
        grid=(n,),
        in_specs=[smem(lambda i: (i, 0, 0)), smem(lambda i: (jnp.minimum(i + 1, n - 1), 0, 0)),
                  pl.BlockSpec(memory_space=pl.ANY),
                  pl.BlockSpec((tm, d), lambda i: (i, 0)),
                  pl.BlockSpec((1, 6, d), lambda i: (i * tm // seq, 0, 0)),
                  pl.BlockSpec((tm, 2), lambda i: (i, 0))],
        out_specs=pl.BlockSpec((tm, d), lambda i: (i, 0)),
        out_shape=jax.ShapeDtypeStruct((t, d), F32),
        scratch_shapes=[pltpu.VMEM((2, 2 * tm // SUBLANES, SUBLANES, yd.shape[1]), yd.dtype),
                        pltpu.SemaphoreType.DMA((2,))],
        compiler_params=_cparams(("arbitrary",)),
        name="combine",
    )(dest3, dest3, yd, x1, mod3, ew_t)


def _layout(counts, n_assign):
    rows = MOE_ROWS
    n_blocks = (n_assign + N_EXPERTS * (rows - 1)) // rows
    padded = (counts + rows - 1) // rows * rows
    pend = jnp.cumsum(padded)
    pstart = pend - padded
    bstart = jnp.arange(n_blocks, dtype=jnp.int32) * rows
    block_e = jnp.minimum(jnp.sum((pend[None, :] <= bstart[:, None]).astype(jnp.int32), axis=1),
                          N_EXPERTS - 1)
    n_valid = jnp.clip(pstart[block_e] + counts[block_e] - bstart, 0, rows)
    n_valid = jnp.where(bstart < pend[-1], n_valid, 0)
    min_used = n_assign // rows
    tail = pend[-1] + jnp.arange(n_blocks - min_used, dtype=jnp.int32) * rows
    zstart = jnp.concatenate([jnp.where(counts > 0, pend - rows, -1),
                              jnp.where(tail < n_blocks * rows, tail, -1)])
    return (pstart.astype(jnp.int32), zstart.astype(jnp.int32), block_e.astype(jnp.int32),
            n_valid.astype(jnp.int32), n_blocks * rows)


def kernel(x, c, positions, ada_w, ada_b, norm1_w, w_in, q_norm_w, k_norm_w, conv_w, conv_b,
           dt_bias, a_log, d_skip, ssm_norm_w, w_out, norm2_w, router_group_w, router_group_b,
           router_expert_w, router_expert_b, w_gate, w_up, w_down):
    batch, seq, d = x.shape
    t = batch * seq
    depth = ada_w.shape[0]
    nh_pad = LANES - N_SSM_HEADS
    inv_freq = ROPE_THETA ** (-(jnp.arange(0, ROT_DIM, 2, dtype=F32) / ROT_DIM))
    freq = jnp.concatenate([inv_freq, inv_freq, jnp.zeros((HEAD_DIM - ROT_DIM,), F32)]).reshape(1, HEAD_DIM)
    pos3 = positions.reshape(batch, seq, 1)
    xc = x.reshape(t, d)
    for l in range(depth):
        mod3 = _adaln(c, ada_w[l], ada_b[l]).reshape(batch, 6, d)
        w_main = w_in[l][:, :N_MAIN].astype(BF16)
        w_dt = jnp.pad(w_in[l][:, N_MAIN:], ((0, 0), (0, nh_pad))).astype(BF16)
        proj, dt_raw = _in_proj(xc, mod3, norm1_w[l].reshape(1, d), w_main, w_dt, seq)
        y_attn = _attention(proj, pos3, freq, q_norm_w[l].reshape(1, HEAD_DIM),
                            k_norm_w[l].reshape(1, HEAD_DIM), batch, seq)
        y_ssm = _ssd(proj, dt_raw, conv_w[l], conv_b[l].reshape(1, -1),
                     jnp.pad(dt_bias[l], (0, nh_pad)).reshape(1, LANES),
                     jnp.pad(a_log[l], (0, nh_pad)).reshape(1, LANES),
                     jnp.repeat(d_skip[l], SSM_HEAD_DIM).reshape(1, D_SSM),
                     ssm_norm_w[l].reshape(1, D_SSM), batch, seq)
        rw = jnp.concatenate([router_expert_w[l].T, router_group_w[l].T,
                              jnp.zeros((LANES - N_EXPERTS - N_EXPERT_GROUPS, d), F32)])
        rb = jnp.concatenate([router_expert_b[l], router_group_b[l],
                              jnp.zeros((LANES - N_EXPERTS - N_EXPERT_GROUPS,), F32)]).reshape(LANES, 1)
        wo = w_out[l].astype(BF16)
        x1, h2, eid3, ew, rank3, counts = _out_router(y_attn, y_ssm, wo[:D_ATTN], wo[D_ATTN:], xc, mod3,
                                                      norm2_w[l].reshape(1, d), rw, rb, seq)
        pstart, zstart, block_e, n_valid, n_rows = _layout(counts.reshape(N_EXPERTS), 2 * t)
        seg = jnp.sum(jnp.where(eid3[..., None] == jnp.arange(N_EXPERTS, dtype=jnp.int32), pstart, 0), axis=-1)
        dest3 = (seg + rank3).reshape(t // OUT_TM, 1, 2 * OUT_TM)
        xd = _scatter_rows(zstart, dest3, h2, n_rows)
        yd = _experts(block_e, n_valid, xd, w_gate[l], w_up[l], w_down[l])
        xc = _combine(dest3, yd, x1, mod3, ew.T, seq)
    return xc.reshape(batch, seq, d)
```

```python
import jax
import jax.numpy as jnp
from jax import lax
from jax.experimental import pallas as pl
from jax.experimental.pallas import tpu as pltpu

F32 = jnp.float32
BF16 = jnp.bfloat16

D_MODEL = 2048
D_ATTN = 1024
D_SSM = 1024
HEAD_DIM = 128
N_ATTN_HEADS = 8
ROT_DIM = 32
ROPE_THETA = 500000.0
ATTN_BLOCK = 128
DILATIONS = (1, 4, 16)
SSM_HEAD_DIM = 64
N_SSM_HEADS = 16
SSM_GROUPS = 8
SSM_STATE = 128
CONV_WIDTH = 4
SSD_CHUNK = 128
N_MAIN = 3 * D_ATTN + D_SSM + D_SSM + 2 * SSM_GROUPS * SSM_STATE
N_EXPERT_GROUPS = 8
EXPERTS_PER_GROUP = 8
N_EXPERTS = 64
D_EXPERT = 512
EPS = 1e-6

LANES = 128
SUBLANES = 8
VMEM_LIMIT = 56 * 1024 * 1024

ADALN_TN = 1024
INPROJ_TM = 1024
INPROJ_TN = 1024
ATTN_GROUP = 5
OUT_TM = 512
MOE_ROWS = 256


def _cparams(sem):
    return pltpu.CompilerParams(dimension_semantics=sem, vmem_limit_bytes=VMEM_LIMIT)


def _pack_bf16_pair(x):
    k = x.shape[1] // 2
    lo = lax.bitcast_convert_type(x[:, :k].astype(BF16).astype(F32), jnp.uint32)
    hi = lax.bitcast_convert_type(x[:, k:].astype(BF16).astype(F32), jnp.uint32)
    return (lo >> 16) | (hi & jnp.uint32(0xFFFF0000))


def _unpack_bf16_pair(p):
    lo = lax.bitcast_convert_type(p << 16, F32)
    hi = lax.bitcast_convert_type(p & jnp.uint32(0xFFFF0000), F32)
    return lo, hi


def _adaln_kernel(c_ref, w_ref, b_ref, o_ref):
    c = c_ref[...]
    ca = (c * jax.nn.sigmoid(c)).astype(BF16)
    o_ref[...] = jnp.dot(ca, w_ref[...].astype(BF16), preferred_element_type=F32) + b_ref[...]


def _adaln(c, ada_w, ada_b):
    b, d = c.shape
    n = ada_w.shape[1]
    return pl.pallas_call(
        _adaln_kernel,
        grid=(n // ADALN_TN,),
        in_specs=[pl.BlockSpec((b, d), lambda j: (0, 0)),
                  pl.BlockSpec((d, ADALN_TN), lambda j: (0, j)),
                  pl.BlockSpec((1, ADALN_TN), lambda j: (0, j))],
        out_specs=pl.BlockSpec((b, ADALN_TN), lambda j: (0, j)),
        out_shape=jax.ShapeDtypeStruct((b, n), F32),
        compiler_params=_cparams(("arbitrary",)),
        name="adaln",
    )(c, ada_w, ada_b.reshape(1, n))


def _modulated_norm(x, nw, shift, scale):
    y = x * lax.rsqrt(jnp.mean(x * x, axis=-1, keepdims=True) + EPS) * nw
    return y * (1.0 + scale) + shift


def _inproj_kernel(x_ref, mod_ref, nw_ref, w_ref, wdt_ref, o_ref, dt_ref, h_scr):
    @pl.when(pl.program_id(1) == 0)
    def _():
        h = _modulated_norm(x_ref[...], nw_ref[...], mod_ref[0, 0:1, :], mod_ref[0, 1:2, :])
        h_scr[...] = h.astype(BF16)
        dt_ref[...] = jnp.dot(h_scr[...], wdt_ref[...], preferred_element_type=F32)

    o_ref[...] = jnp.dot(h_scr[...], w_ref[...], preferred_element_type=F32).astype(BF16)


def _in_proj(x2, mod3, norm_w, w_main, w_dt, seq):
    t, d = x2.shape
    tm, tn = INPROJ_TM, INPROJ_TN
    return pl.pallas_call(
        _inproj_kernel,
        grid=(t // tm, N_MAIN // tn),
        in_specs=[pl.BlockSpec((tm, d), lambda i, j: (i, 0)),
                  pl.BlockSpec((1, 6, d), lambda i, j: (i * tm // seq, 0, 0)),
                  pl.BlockSpec((1, d), lambda i, j: (0, 0)),
                  pl.BlockSpec((d, tn), lambda i, j: (0, j)),
                  pl.BlockSpec((d, LANES), lambda i, j: (0, 0))],
        out_specs=[pl.BlockSpec((tm, tn), lambda i, j: (i, j)),
                   pl.BlockSpec((tm, LANES), lambda i, j: (i, 0))],
        out_shape=[jax.ShapeDtypeStruct((t, N_MAIN), BF16),
                   jax.ShapeDtypeStruct((t, LANES), F32)],
        scratch_shapes=[pltpu.VMEM((tm, d), BF16)],
        compiler_params=_cparams(("parallel", "arbitrary")),
        name="in_proj",
    )(x2, mod3, norm_w, w_main, w_dt)


def _attn_kernel(q_ref, k_ref, v_ref, pos_ref, freq_ref, qw_ref, kw_ref, o_ref,
                 cos_s, sa_s, sb_s, qs, ks, vs, acc_s, m_s, l_s, tmp):
    seq = q_ref.shape[0]
    blk = ATTN_BLOCK
    half = ROT_DIM // 2

    @pl.when(pl.program_id(1) == 0)
    def _():
        ang = pos_ref[0].astype(F32) * freq_ref[...]
        lane = lax.broadcasted_iota(jnp.int32, (seq, LANES), 1)
        sn = jnp.sin(ang)
        cos_s[...] = jnp.cos(ang)
        sa_s[...] = jnp.where(lane < half, -sn, 0.0)
        sb_s[...] = jnp.where((lane >= half) & (lane < ROT_DIM), sn, 0.0)

    def norm_rope(t_ref, w_ref):
        t = t_ref[...].astype(F32)
        y = t * lax.rsqrt(jnp.mean(t * t, axis=-1, keepdims=True) + EPS) * w_ref[...]
        return (y * cos_s[...] + pltpu.roll(y, LANES - half, 1) * sa_s[...]
                + pltpu.roll(y, half, 1) * sb_s[...])

    qseg = seq // 4
    slab = blk // 4

    def to_quarter(val, dst):
        tmp[...] = val
        for r in range(4):
            dst[r * qseg:(r + 1) * qseg, :] = tmp[pl.ds(r, qseg, stride=4), :]

    to_quarter(norm_rope(q_ref, qw_ref), qs)
    to_quarter(norm_rope(k_ref, kw_ref), ks)
    to_quarter(v_ref[...].astype(F32), vs)

    qi = lax.broadcasted_iota(jnp.int32, (blk, blk), 0)
    kj = lax.broadcasted_iota(jnp.int32, (blk, blk), 1)
    pq = 4 * (qi % slab) + qi // slab
    pk = 4 * (kj % slab) + kj // slab
    masks = {
        "seq": (kj <= qi, jnp.concatenate([kj >= qi, kj <= qi], axis=1)),
        "slab": (pk <= pq, jnp.concatenate([pk >= pq, pk <= pq], axis=1)),
    }
    ones_blk = jnp.ones((blk, LANES), BF16)
    scale = HEAD_DIM ** -0.5
    nt = (((1,), (1,)), ((), ()))

    def load(ref, kind, start):
        if kind == "c":
            return ref[pl.ds(start, blk), :]
        if kind == "s":
            return ref[pl.ds(start, blk, stride=4), :]
        return jnp.concatenate([ref[pl.ds(start + r * qseg, slab), :] for r in range(4)], axis=0)

    def store(ref, kind, start, val):
        if kind == "c":
            ref[pl.ds(start, blk), :] = val
        elif kind == "s":
            ref[pl.ds(start, blk, stride=4), :] = val
        else:
            for r in range(4):
                ref[pl.ds(start + r * qseg, slab), :] = val[r * slab:(r + 1) * slab]

    def group(kind, starts, with_prev, first_pattern):
        step = slab if kind == "q" else blk
        qb = [load(qs, kind, s).astype(BF16) for s in starts]
        kb = [load(ks, kind, s).astype(BF16) for s in starts]
        vb = [jnp.concatenate([load(vs, kind, s).astype(BF16), ones_blk], axis=1) for s in starts]
        if with_prev:
            kb = [jnp.concatenate([load(ks, kind, s - step).astype(BF16), k], axis=0)
                  for s, k in zip(starts, kb)]
            vb = [jnp.concatenate([jnp.concatenate([load(vs, kind, s - step).astype(BF16), ones_blk], axis=1),
                                   v], axis=0) for s, v in zip(starts, vb)]
        mask = masks["slab" if kind == "q" else "seq"][1 if with_prev else 0]
        sc = [jnp.where(mask, lax.dot_general(q, k, nt, preferred_element_type=F32) * scale, -jnp.inf)
              for q, k in zip(qb, kb)]
        if with_prev:
            mx = [jnp.max(jnp.maximum(x[:, :blk], x[:, blk:]), axis=-1, keepdims=True) for x in sc]
        else:
            mx = [jnp.max(x, axis=-1, keepdims=True) for x in sc]
        pr = [jnp.exp(x - m).astype(BF16) for x, m in zip(sc, mx)]
        ov = [jnp.dot(p, v, preferred_element_type=F32) for p, v in zip(pr, vb)]
        mb = [jnp.broadcast_to(m, (blk, LANES)) for m in mx]
        if first_pattern:
            new = [(m, o[:, LANES:], o[:, :LANES]) for m, o in zip(mb, ov)]
        else:
            old = [(load(m_s, kind, s), load(l_s, kind, s), load(acc_s, kind, s)) for s in starts]
            new = []
            for (m_old, l_old, acc_old), m, o in zip(old, mb, ov):
                m_new = jnp.maximum(m_old, m)
                a_old = jnp.exp(m_old - m_new)
                a_new = jnp.exp(m - m_new)
                new.append((m_new, a_old * l_old + a_new * o[:, LANES:],
                            a_old * acc_old + a_new * o[:, :LANES]))
        for s, (m_new, l_new, acc_new) in zip(starts, new):
            store(m_s, kind, s, m_new)
            store(l_s, kind, s, l_new)
            store(acc_s, kind, s, acc_new)

    def loop(n_iter, kind, starts_of, with_prev, first_pattern):
        def body(i, carry):
            group(kind, starts_of(i), with_prev, first_pattern)
            return carry
        lax.fori_loop(0, n_iter, body, 0)

    nb4 = qseg // blk
    group("c", [r * qseg for r in range(4)], False, True)
    loop(2, "c", lambda i: [(2 * i + j) * qseg + n * blk for j in range(2) for n in range(1, nb4)], True, True)
    loop(4, "s", lambda r: [r * qseg + a for a in range(4)], False, False)
    group("q", [0], False, False)
    loop((seq // blk - 1) // ATTN_GROUP, "q",
         lambda i: [(1 + ATTN_GROUP * i + j) * slab for j in range(ATTN_GROUP)], True, False)

    tmp[...] = acc_s[...] / l_s[...]
    for r in range(4):
        qs[pl.ds(r, qseg, stride=4), :] = tmp[r * qseg:(r + 1) * qseg, :]
    o_ref[...] = qs[...].astype(BF16)


def _attention(proj, pos3, freq, qw, kw, batch, seq):
    t = proj.shape[0]
    hd = HEAD_DIM
    nh = N_ATTN_HEADS
    assert seq % (ATTN_BLOCK * max(DILATIONS)) == 0 and (seq // ATTN_BLOCK - 1) % ATTN_GROUP == 0
    scr = pltpu.VMEM((seq, hd), F32)
    return pl.pallas_call(
        _attn_kernel,
        grid=(batch, nh),
        in_specs=[pl.BlockSpec((seq, hd), lambda b, h: (b, h)),
                  pl.BlockSpec((seq, hd), lambda b, h: (b, nh + h)),
                  pl.BlockSpec((seq, hd), lambda b, h: (b, 2 * nh + h)),
                  pl.BlockSpec((1, seq, 1), lambda b, h: (b, 0, 0)),
                  pl.BlockSpec((1, hd), lambda b, h: (0, 0)),
                  pl.BlockSpec((1, hd), lambda b, h: (0, 0)),
                  pl.BlockSpec((1, hd), lambda b, h: (0, 0))],
        out_specs=pl.BlockSpec((seq, hd), lambda b, h: (b, h)),
        out_shape=jax.ShapeDtypeStruct((t, D_ATTN), BF16),
        scratch_shapes=[scr] * 10,
        compiler_params=_cparams(("parallel", "arbitrary")),
        name="attention",
    )(proj, proj, proj, pos3, freq, qw, kw)


def _split3(v):
    hi = v.astype(BF16)
    r1 = v - hi.astype(F32)
    mid = r1.astype(BF16)
    lo = (r1 - mid.astype(F32)).astype(BF16)
    return hi, mid, lo


def _ssd_kernel(xs_ref, bm_ref, cm_ref, z_ref, dtr_ref, cwx_ref, cwb_ref, cwc_ref,
                cbx_ref, cbb_ref, cbc_ref, dtb_ref, alog_ref, dsk_ref, nw_ref, o_ref,
                xpad, x_s, b_s, c_s, dt_s, da_s, dtb_s, daf_s):
    seq = xs_ref.shape[0]
    ck = SSD_CHUNK
    g = pl.program_id(1)
    pad = SUBLANES

    def conv_silu(src_ref, w_ref, b_ref, dst):
        xpad[0:pad, :] = jnp.zeros((pad, LANES), F32)
        xpad[pad:, :] = src_ref[...].astype(F32)
        y = b_ref[...] + jnp.zeros((seq, LANES), F32)
        for j in range(CONV_WIDTH):
            y = y + w_ref[j:j + 1, :] * xpad[pl.ds(pad - (CONV_WIDTH - 1) + j, seq), :]
        dst[...] = y * jax.nn.sigmoid(y)

    conv_silu(xs_ref, cwx_ref, cbx_ref, x_s)
    conv_silu(bm_ref, cwb_ref, cbb_ref, b_s)
    conv_silu(cm_ref, cwc_ref, cbc_ref, c_s)

    @pl.when(g == 0)
    def _():
        xr = dtr_ref[...] + dtb_ref[...]
        dt = jnp.maximum(xr, 0.0) + jnp.log1p(jnp.exp(-jnp.abs(xr)))
        for i, piece in enumerate(_split3(dt)):
            dt_s[i] = piece
        for i, piece in enumerate(_split3(dt * (-jnp.exp(alog_ref[...])))):
            da_s[i] = piece

    def select(v_s, i, sel):
        return jnp.dot(v_s[i], sel, preferred_element_type=F32)

    hrow = lax.broadcasted_iota(jnp.int32, (LANES, 2 * LANES), 0)
    hlane = lax.broadcasted_iota(jnp.int32, (LANES, 2 * LANES), 1)
    brow = lax.broadcasted_iota(jnp.int32, (LANES, LANES), 0)
    blane = lax.broadcasted_iota(jnp.int32, (LANES, LANES), 1)
    sel_b = (brow == 2 * g + (blane >= SSM_HEAD_DIM).astype(jnp.int32)).astype(BF16)
    sel_f = (hrow == 2 * g + (hlane >= LANES).astype(jnp.int32)).astype(BF16)
    dtb_s[...] = select(dt_s, 0, sel_b) + select(dt_s, 1, sel_b) + select(dt_s, 2, sel_b)
    for i in range(3):
        daf_s[i] = select(da_s, i, sel_f).astype(BF16)

    li = lax.broadcasted_iota(jnp.int32, (ck, ck), 0)
    si = lax.broadcasted_iota(jnp.int32, (ck, ck), 1)
    causal = si <= li
    eye = si == li
    tril_bf = causal.astype(BF16)
    head1 = lax.broadcasted_iota(jnp.int32, (ck, LANES), 1) >= SSM_HEAD_DIM
    head1_row = lax.broadcasted_iota(jnp.int32, (1, LANES), 1) >= SSM_HEAD_DIM

    def chunk(c, prev):
        r = pl.ds(pl.multiple_of(c * ck, ck), ck)
        xc = x_s[r, :]
        bc = b_s[r, :].astype(BF16)
        cc = c_s[r, :].astype(BF16)
        acs2 = (jnp.dot(tril_bf, daf_s[0, r, :], preferred_element_type=F32)
                + jnp.dot(tril_bf, daf_s[1, r, :], preferred_element_type=F32)
                + jnp.dot(tril_bf, daf_s[2, r, :], preferred_element_type=F32))
        acs_f = (acs2[:, :LANES], acs2[:, LANES:])
        acs_b = jnp.where(head1, acs_f[1], acs_f[0])
        last_b = jnp.where(head1_row, acs_f[1][ck - 1:ck, :], acs_f[0][ck - 1:ck, :])
        cb = lax.dot_general(cc, bc, (((1,), (1,)), ((), ())), preferred_element_type=F32)
        xdt = xc * dtb_s[r, :]
        scores = []
        for k in range(2):
            acs_row = jnp.sum(jnp.where(eye, acs_f[k], 0.0), axis=0, keepdims=True)
            decay = jnp.exp(jnp.where(causal, acs_f[k] - acs_row, -jnp.inf))
            scores.append((cb * decay).astype(BF16))
        xdt2 = jnp.concatenate([jnp.where(head1, 0.0, xdt).astype(BF16),
                                jnp.where(head1, xdt, 0.0).astype(BF16)], axis=0)
        y_diag = jnp.dot(jnp.concatenate(scores, axis=1), xdt2, preferred_element_type=F32)
        y_off = jnp.dot(cc, prev.astype(BF16), preferred_element_type=F32) * jnp.exp(acs_b)
        y = y_diag + y_off + dsk_ref[...] * xc
        zc = z_ref[r, :].astype(F32)
        y = y * (zc * jax.nn.sigmoid(zc))
        y = y * lax.rsqrt(jnp.mean(y * y, axis=-1, keepdims=True) + EPS) * nw_ref[...]
        o_ref[r, :] = y.astype(BF16)
        w = (xdt * jnp.exp(last_b - acs_b)).astype(BF16)
        st = lax.dot_general(bc, w, (((0,), (0,)), ((), ())), preferred_element_type=F32)
        return prev * jnp.exp(last_b) + st

    lax.fori_loop(0, seq // ck, chunk, jnp.zeros((SSM_STATE, LANES), F32), unroll=8)


def _ssd(proj, dt_raw, conv_w, conv_b, dtb, alog, dsk, nw, batch, seq):
    t = proj.shape[0]
    gs = SSM_GROUPS
    col = lambda off: (lambda b, g: (b, off + g))
    wcol = lambda off: (lambda b, g: (0, off + g))
    zq, xq, bq, cq = 3 * D_ATTN // LANES, 4 * D_ATTN // LANES, 5 * D_ATTN // LANES, 6 * D_ATTN // LANES
    const = lambda b, g: (0, 0)
    scr = pltpu.VMEM((seq, LANES), F32)
    return pl.pallas_call(
        _ssd_kernel,
        grid=(batch, gs),
        in_specs=[pl.BlockSpec((seq, LANES), col(xq)),
                  pl.BlockSpec((seq, LANES), col(bq)),
                  pl.BlockSpec((seq, LANES), col(cq)),
                  pl.BlockSpec((seq, LANES), col(zq)),
                  pl.BlockSpec((seq, LANES), lambda b, g: (b, 0)),
                  pl.BlockSpec((CONV_WIDTH, LANES), wcol(0)),
                  pl.BlockSpec((CONV_WIDTH, LANES), wcol(gs)),
                  pl.BlockSpec((CONV_WIDTH, LANES), wcol(2 * gs)),
                  pl.BlockSpec((1, LANES), wcol(0)),
                  pl.BlockSpec((1, LANES), wcol(gs)),
                  pl.BlockSpec((1, LANES), wcol(2 * gs)),
                  pl.BlockSpec((1, LANES), const),
                  pl.BlockSpec((1, LANES), const),
                  pl.BlockSpec((1, LANES), lambda b, g: (0, g)),
                  pl.BlockSpec((1, LANES), lambda b, g: (0, g))],
        out_specs=pl.BlockSpec((seq, LANES), lambda b, g: (b, g)),
        out_shape=jax.ShapeDtypeStruct((t, D_SSM), BF16),
        scratch_shapes=[pltpu.VMEM((seq + SUBLANES, LANES), F32), scr, scr, scr,
                        pltpu.VMEM((3, seq, LANES), BF16), pltpu.VMEM((3, seq, LANES), BF16), scr,
                        pltpu.VMEM((3, seq, 2 * LANES), BF16)],
        compiler_params=_cparams(("arbitrary", "arbitrary")),
        name="ssd",
    )(proj, proj, proj, proj, dt_raw, conv_w, conv_w, conv_w, conv_b, conv_b, conv_b,
      dtb, alog, dsk, nw)


def _out_router_kernel(ya_ref, ys_ref, wa_ref, ws_ref, x_ref, mod_ref, nw_ref, rw_ref, rb_ref,
                       x1_ref, h2_ref, eid_ref, ew_ref, rank_ref, cnt_ref, cnt_s):
    tm = x_ref.shape[0]
    mix = (jnp.dot(ya_ref[...], wa_ref[...], preferred_element_type=F32)
           + jnp.dot(ys_ref[...], ws_ref[...], preferred_element_type=F32))
    x1 = x_ref[...] + mod_ref[0, 2:3, :] * mix
    x1_ref[...] = x1
    h2 = _modulated_norm(x1, nw_ref[...], mod_ref[0, 3:4, :], mod_ref[0, 4:5, :])
    h2_ref[...] = _pack_bf16_pair(h2)

    h_hi = h2.astype(BF16)
    h_lo = (h2 - h_hi.astype(F32)).astype(BF16)
    rw = rw_ref[...]
    w_hi = rw.astype(BF16)
    w_lo = (rw - w_hi.astype(F32)).astype(BF16)
    nt = (((1,), (1,)), ((), ()))
    logits = (lax.dot_general(w_hi, h_hi, nt, preferred_element_type=F32)
              + lax.dot_general(w_hi, h_lo, nt, preferred_element_type=F32)
              + lax.dot_general(w_lo, h_hi, nt, preferred_element_type=F32)) + rb_ref[...]

    ng, ne = N_EXPERT_GROUPS, EXPERTS_PER_GROUP
    iota = lax.broadcasted_iota(jnp.int32, (ne, tm), 0)

    def argmax0(v):
        mx = jnp.max(v, axis=0, keepdims=True)
        idx = jnp.min(jnp.where(v == mx, iota, ne), axis=0, keepdims=True)
        return mx, idx

    gl = logits[N_EXPERTS:N_EXPERTS + ng, :]
    gmax, gidx = argmax0(gl)
    g_gate = 1.0 / jnp.sum(jnp.exp(gl - gmax), axis=0, keepdims=True)
    e_in = jnp.zeros((ne, tm), F32)
    for gi in range(ng):
        e_in = e_in + jnp.where(gidx == gi, logits[gi * ne:(gi + 1) * ne, :], 0.0)
    m1, i1 = argmax0(e_in)
    m2, i2 = argmax0(jnp.where(iota == i1, -jnp.inf, e_in))
    p2 = jnp.exp(m2 - m1)
    den = 1.0 + p2
    e0 = gidx * ne + i1
    e1 = gidx * ne + i2
    eid_ref[0, 0:1, :] = e0
    eid_ref[0, 1:2, :] = e1
    ew_ref[0:1, :] = (1.0 / den) * g_gate
    ew_ref[1:2, :] = (p2 / den) * g_gate

    @pl.when(pl.program_id(0) == 0)
    def _():
        cnt_s[...] = jnp.zeros(cnt_s.shape, F32)

    eiota = lax.broadcasted_iota(jnp.int32, (N_EXPERTS, tm), 0)
    oh0 = eiota == e0
    oh1 = eiota == e1
    both = jnp.logical_or(oh0, oh1)
    ti = lax.broadcasted_iota(jnp.int32, (tm, tm), 0)
    tj = lax.broadcasted_iota(jnp.int32, (tm, tm), 1)
    earlier = (ti < tj).astype(BF16)
    before = cnt_s[...] + jnp.dot(both.astype(BF16), earlier, preferred_element_type=F32)
    rank_ref[0, 0:1, :] = jnp.sum(jnp.where(oh0, before, 0.0), axis=0, keepdims=True).astype(jnp.int32)
    rank_ref[0, 1:2, :] = jnp.sum(jnp.where(oh1, before, 0.0), axis=0, keepdims=True).astype(jnp.int32)
    total = cnt_s[...] + jnp.sum(both.astype(F32), axis=1, keepdims=True)
    cnt_s[...] = total
    cnt_ref[...] = total.astype(jnp.int32)


def _out_router(ya, ys, wa, ws, x2, mod3, norm_w, rw, rb, seq):
    t, d = x2.shape
    tm = OUT_TM
    const = lambda i: (0, 0)
    return pl.pallas_call(
        _out_router_kernel,
        grid=(t // tm,),
        in_specs=[pl.BlockSpec((tm, D_ATTN), lambda i: (i, 0)),
                  pl.BlockSpec((tm, D_SSM), lambda i: (i, 0)),
                  pl.BlockSpec((D_ATTN, d), const, pipeline_mode=pl.Buffered(1)),
                  pl.BlockSpec((D_SSM, d), const, pipeline_mode=pl.Buffered(1)),
                  pl.BlockSpec((tm, d), lambda i: (i, 0)),
                  pl.BlockSpec((1, 6, d), lambda i: (i * tm // seq, 0, 0)),
                  pl.BlockSpec((1, d), const),
                  pl.BlockSpec((LANES, d), const, pipeline_mode=pl.Buffered(1)),
                  pl.BlockSpec((LANES, 1), const)],
        out_specs=[pl.BlockSpec((tm, d), lambda i: (i, 0)),
                   pl.BlockSpec((tm, d // 2), lambda i: (i, 0)),
                   pl.BlockSpec((1, 2, tm), lambda i: (i, 0, 0)),
                   pl.BlockSpec((2, tm), lambda i: (0, i)),
                   pl.BlockSpec((1, 2, tm), lambda i: (i, 0, 0)),
                   pl.BlockSpec((N_EXPERTS, 1), const)],
        out_shape=[jax.ShapeDtypeStruct((t, d), F32),
                   jax.ShapeDtypeStruct((t, d // 2), jnp.uint32),
                   jax.ShapeDtypeStruct((t // tm, 2, tm), jnp.int32),
                   jax.ShapeDtypeStruct((2, t), F32),
                   jax.ShapeDtypeStruct((t // tm, 2, tm), jnp.int32),
                   jax.ShapeDtypeStruct((N_EXPERTS, 1), jnp.int32)],
        scratch_shapes=[pltpu.VMEM((N_EXPERTS, 1), F32)],
        compiler_params=_cparams(("arbitrary",)),
        name="out_router",
    )(ya, ys, wa, ws, x2, mod3, norm_w, rw, rb)


def _for_each_assignment(tm, body):
    def group(gi, carry):
        for jj in range(SUBLANES):
            for k in range(2):
                body(k, gi, jj, k * tm + gi * SUBLANES + jj)
        return carry
    lax.fori_loop(0, tm // SUBLANES, group, 0)


def _rows3(a):
    return a.reshape(a.shape[0] // SUBLANES, SUBLANES, a.shape[1])


def _scatter_kernel(zstart_ref, dest_ref, h_ref, xd_hbm, sem, zbuf, zsem):
    tm = h_ref.shape[0] * SUBLANES

    @pl.when(pl.program_id(0) == 0)
    def _():
        zbuf[...] = jnp.zeros(zbuf.shape, zbuf.dtype)

        def zero_copy(j):
            start = pl.multiple_of(jnp.maximum(zstart_ref[j], 0), MOE_ROWS)
            return pltpu.make_async_copy(zbuf, xd_hbm.at[pl.ds(start, MOE_ROWS)], zsem)

        def z_issue(j, carry):
            @pl.when(zstart_ref[j] >= 0)
            def _():
                zero_copy(j).start()
            return carry

        def z_wait(j, carry):
            @pl.when(zstart_ref[j] >= 0)
            def _():
                zero_copy(j).wait()
            return carry

        lax.fori_loop(0, zstart_ref.shape[0], z_issue, 0)
        lax.fori_loop(0, zstart_ref.shape[0], z_wait, 0)

    def issue(k, gi, jj, j):
        pltpu.make_async_copy(h_ref.at[gi, pl.ds(jj, 1)], xd_hbm.at[pl.ds(dest_ref[0, 0, j], 1)], sem).start()

    _for_each_assignment(tm, issue)
    for _ in range(2 * SUBLANES):
        pltpu.make_async_copy(h_ref.at[:, 0], xd_hbm.at[pl.ds(0, tm // SUBLANES)], sem).wait()


def _scatter_rows(zstart, dest3, h2, n_rows):
    t, d = h2.shape
    n_tiles = dest3.shape[0]
    tm = dest3.shape[2] // 2
    grid_spec = pltpu.PrefetchScalarGridSpec(
        num_scalar_prefetch=1,
        grid=(n_tiles,),
        in_specs=[pl.BlockSpec((1, 1, 2 * tm), lambda i, zs: (i, 0, 0), memory_space=pltpu.SMEM),
                  pl.BlockSpec((tm // SUBLANES, SUBLANES, d), lambda i, zs: (i, 0, 0))],
        out_specs=pl.BlockSpec(memory_space=pl.ANY),
        scratch_shapes=[pltpu.SemaphoreType.DMA(()), pltpu.VMEM((MOE_ROWS, d), h2.dtype),
                        pltpu.SemaphoreType.DMA(())],
    )
    return pl.pallas_call(
        _scatter_kernel,
        grid_spec=grid_spec,
        out_shape=jax.ShapeDtypeStruct((n_rows, d), h2.dtype),
        compiler_params=_cparams(("arbitrary",)),
        name="scatter_rows",
    )(zstart, dest3, _rows3(h2))


def _experts_kernel(be_ref, nv_ref, first_ref, nxt_ref, slot_ref, x_ref, wg_hbm, wu_hbm, wd_hbm, o_ref,
                    wg_f, wu_f, wd_f, sems, wg_s, wu_s, wd_s):
    i = pl.program_id(0)
    n_valid = nv_ref[i]
    half = x_ref.shape[1]

    def weight_copies(e, slot):
        return (pltpu.make_async_copy(wg_hbm.at[e], wg_f.at[slot], sems.at[slot, 0]),
                pltpu.make_async_copy(wu_hbm.at[e], wu_f.at[slot], sems.at[slot, 1]),
                pltpu.make_async_copy(wd_hbm.at[e], wd_f.at[slot], sems.at[slot, 2]))

    @pl.when(i == 0)
    def _():
        for c in weight_copies(be_ref[0], slot_ref[0]):
            c.start()

    @pl.when(first_ref[i] == 1)
    def _():
        slot = slot_ref[i]
        for c in weight_copies(be_ref[i], slot):
            c.wait()

        @pl.when(nxt_ref[i] >= 0)
        def _():
            for c in weight_copies(nxt_ref[i], 1 - slot):
                c.start()

        wg_s[...] = wg_f[slot].astype(BF16)
        wu_s[...] = wu_f[slot].astype(BF16)
        wd_s[...] = wd_f[slot].astype(BF16)

    @pl.when(n_valid > 0)
    def _():
        x_lo, x_hi = _unpack_bf16_pair(x_ref[...])
        x_lo = x_lo.astype(BF16)
        x_hi = x_hi.astype(BF16)
        gt = (jnp.dot(x_lo, wg_s[0:half, :], preferred_element_type=F32)
              + jnp.dot(x_hi, wg_s[half:, :], preferred_element_type=F32))
        up = (jnp.dot(x_lo, wu_s[0:half, :], preferred_element_type=F32)
              + jnp.dot(x_hi, wu_s[half:, :], preferred_element_type=F32))
        hid = (gt * jax.nn.sigmoid(gt)) * up
        o_ref[...] = _pack_bf16_pair(jnp.dot(hid.astype(BF16), wd_s[...], preferred_element_type=F32))

    @pl.when(n_valid == 0)
    def _():
        o_ref[...] = jnp.zeros(o_ref.shape, o_ref.dtype)


def _experts(block_e, n_valid, first, nxt, slot, xd, w_gate, w_up, w_down):
    rows = MOE_ROWS
    n_blocks = xd.shape[0] // rows
    half = xd.shape[1]
    d = 2 * half
    de = w_gate.shape[2]
    blk = lambda i, *_: (i, 0)
    grid_spec = pltpu.PrefetchScalarGridSpec(
        num_scalar_prefetch=5,
        grid=(n_blocks,),
        in_specs=[pl.BlockSpec((rows, half), blk),
                  pl.BlockSpec(memory_space=pl.ANY),
                  pl.BlockSpec(memory_space=pl.ANY),
                  pl.BlockSpec(memory_space=pl.ANY)],
        out_specs=pl.BlockSpec((rows, half), blk),
        scratch_shapes=[pltpu.VMEM((2, d, de), F32),
                        pltpu.VMEM((2, d, de), F32),
                        pltpu.VMEM((2, de, d), F32),
                        pltpu.SemaphoreType.DMA((2, 3)),
                        pltpu.VMEM((d, de), BF16),
                        pltpu.VMEM((d, de), BF16),
                        pltpu.VMEM((de, d), BF16)],
    )
    return pl.pallas_call(
        _experts_kernel,
        grid_spec=grid_spec,
        out_shape=jax.ShapeDtypeStruct((n_blocks * rows, half), jnp.uint32),
        compiler_params=_cparams(("arbitrary",)),
        name="experts",
    )(block_e, n_valid, first, nxt, slot, xd, w_gate, w_up, w_down)


def _combine_kernel(dest_ref, destn_ref, yd_hbm, x1_ref, mod_ref, ew_ref, o_ref, ybuf, sems):
    i = pl.program_id(0)
    n = pl.num_programs(0)
    tm = x1_ref.shape[0]
    tg = tm // SUBLANES
    slot = i % 2

    def gather(d_ref, s):
        def issue(k, gi, jj, j):
            pltpu.make_async_copy(yd_hbm.at[pl.ds(d_ref[0, 0, j], 1)],
                                  ybuf.at[s, k * tg + gi, pl.ds(jj, 1)], sems.at[s]).start()
        _for_each_assignment(tm, issue)

    @pl.when(i == 0)
    def _():
        gather(dest_ref, 0)

    @pl.when(i + 1 < n)
    def _():
        gather(destn_ref, 1 - slot)

    for _ in range(SUBLANES):
        pltpu.make_async_copy(yd_hbm.at[pl.ds(0, 2 * tg)], ybuf.at[slot, :, 0], sems.at[slot]).wait()
    half = ybuf.shape[3]
    y0_lo, y0_hi = _unpack_bf16_pair(ybuf[slot, 0:tg].reshape(tm, half))
    y1_lo, y1_hi = _unpack_bf16_pair(ybuf[slot, tg:2 * tg].reshape(tm, half))
    w0 = ew_ref[:, 0:1]
    w1 = ew_ref[:, 1:2]
    o_ref[:, 0:half] = x1_ref[:, 0:half] + mod_ref[0, 5:6, 0:half] * (w0 * y0_lo + w1 * y1_lo)
    o_ref[:, half:] = x1_ref[:, half:] + mod_ref[0, 5:6, half:] * (w0 * y0_hi + w1 * y1_hi)


def _combine(dest3, yd, x1, mod3, ew_t, seq):
    t, d = x1.shape
    n = dest3.shape[0]
    tm = dest3.shape[2] // 2
    smem = lambda im: pl.BlockSpec((1, 1, 2 * tm), im, memory_space=pltpu.SMEM)
    return pl.pallas_call(
        _combine_kernel,
        grid=(n,),
        in_specs=[smem(lambda i: (i, 0, 0)), smem(lambda i: (jnp.minimum(i + 1, n - 1), 0, 0)),
                  pl.BlockSpec(memory_space=pl.ANY),
                  pl.BlockSpec((tm, d), lambda i: (i, 0)),
                  pl.BlockSpec((1, 6, d), lambda i: (i * tm // seq, 0, 0)),
                  pl.BlockSpec((tm, 2), lambda i: (i, 0))],
        out_specs=pl.BlockSpec((tm, d), lambda i: (i, 0)),
        out_shape=jax.ShapeDtypeStruct((t, d), F32),
        scratch_shapes=[pltpu.VMEM((2, 2 * tm // SUBLANES, SUBLANES, yd.shape[1]), yd.dtype),
                        pltpu.SemaphoreType.DMA((2,))],
        compiler_params=_cparams(("arbitrary",)),
        name="combine",
    )(dest3, dest3, yd, x1, mod3, ew_t)


def _layout(counts, n_assign):
    rows = MOE_ROWS
    n_blocks = (n_assign + N_EXPERTS * (rows - 1)) // rows
    padded = (counts + rows - 1) // rows * rows
    pend = jnp.cumsum(padded)
    pstart = pend - padded
    bstart = jnp.arange(n_blocks, dtype=jnp.int32) * rows
    block_e = jnp.minimum(jnp.sum((pend[None, :] <= bstart[:, None]).astype(jnp.int32), axis=1),
                          N_EXPERTS - 1)
    n_valid = jnp.clip(pstart[block_e] + counts[block_e] - bstart, 0, rows)
    n_valid = jnp.where(bstart < pend[-1], n_valid, 0)
    min_used = n_assign // rows
    tail = pend[-1] + jnp.arange(n_blocks - min_used, dtype=jnp.int32) * rows
    zstart = jnp.concatenate([jnp.where(counts > 0, pend - rows, -1),
                              jnp.where(tail < n_blocks * rows, tail, -1)])
    first = (n_valid > 0) & jnp.concatenate([jnp.ones((1,), bool), block_e[1:] != block_e[:-1]])
    slot = (jnp.cumsum(first.astype(jnp.int32)) - 1) % 2
    ids = jnp.where(counts > 0, jnp.arange(N_EXPERTS, dtype=jnp.int32), N_EXPERTS)
    later = jnp.concatenate([lax.cummin(ids[::-1])[::-1][1:], jnp.full((1,), N_EXPERTS, jnp.int32)])
    nxt = jnp.where(later < N_EXPERTS, later, -1)[block_e]
    sched = (first.astype(jnp.int32), nxt.astype(jnp.int32), jnp.maximum(slot, 0).astype(jnp.int32))
    return (pstart.astype(jnp.int32), zstart.astype(jnp.int32), block_e.astype(jnp.int32),
            n_valid.astype(jnp.int32), sched, n_blocks * rows)


def kernel(x, c, positions, ada_w, ada_b, norm1_w, w_in, q_norm_w, k_norm_w, conv_w, conv_b,
           dt_bias, a_log, d_skip, ssm_norm_w, w_out, norm2_w, router_group_w, router_group_b,
           router_expert_w, router_expert_b, w_gate, w_up, w_down):
    batch, seq, d = x.shape
    t = batch * seq
    depth = ada_w.shape[0]
    nh_pad = LANES - N_SSM_HEADS
    inv_freq = ROPE_THETA ** (-(jnp.arange(0, ROT_DIM, 2, dtype=F32) / ROT_DIM))
    freq = jnp.concatenate([inv_freq, inv_freq, jnp.zeros((HEAD_DIM - ROT_DIM,), F32)]).reshape(1, HEAD_DIM)
    pos3 = positions.reshape(batch, seq, 1)
    xc = x.reshape(t, d)
    for l in range(depth):
        mod3 = _adaln(c, ada_w[l], ada_b[l]).reshape(batch, 6, d)
        w_main = w_in[l][:, :N_MAIN].astype(BF16)
        w_dt = jnp.pad(w_in[l][:, N_MAIN:], ((0, 0), (0, nh_pad))).astype(BF16)
        proj, dt_raw = _in_proj(xc, mod3, norm1_w[l].reshape(1, d), w_main, w_dt, seq)
        y_attn = _attention(proj, pos3, freq, q_norm_w[l].reshape(1, HEAD_DIM),
                            k_norm_w[l].reshape(1, HEAD_DIM), batch, seq)
        y_ssm = _ssd(proj, dt_raw, conv_w[l], conv_b[l].reshape(1, -1),
                     jnp.pad(dt_bias[l], (0, nh_pad)).reshape(1, LANES),
                     jnp.pad(a_log[l], (0, nh_pad)).reshape(1, LANES),
                     jnp.repeat(d_skip[l], SSM_HEAD_DIM).reshape(1, D_SSM),
                     ssm_norm_w[l].reshape(1, D_SSM), batch, seq)
        rw = jnp.concatenate([router_expert_w[l].T, router_group_w[l].T,
                              jnp.zeros((LANES - N_EXPERTS - N_EXPERT_GROUPS, d), F32)])
        rb = jnp.concatenate([router_expert_b[l], router_group_b[l],
                              jnp.zeros((LANES - N_EXPERTS - N_EXPERT_GROUPS,), F32)]).reshape(LANES, 1)
        wo = w_out[l].astype(BF16)
        x1, h2, eid3, ew, rank3, counts = _out_router(y_attn, y_ssm, wo[:D_ATTN], wo[D_ATTN:], xc, mod3,
                                                      norm2_w[l].reshape(1, d), rw, rb, seq)
        pstart, zstart, block_e, n_valid, sched, n_rows = _layout(counts.reshape(N_EXPERTS), 2 * t)
        seg = jnp.sum(jnp.where(eid3[..., None] == jnp.arange(N_EXPERTS, dtype=jnp.int32), pstart, 0), axis=-1)
        dest3 = (seg + rank3).reshape(t // OUT_TM, 1, 2 * OUT_TM)
        xd = _scatter_rows(zstart, dest3, h2, n_rows)
        yd = _experts(block_e, n_valid, *sched, xd, w_gate[l], w_up[l], w_down[l])
        xc = _combine(dest3, yd, x1, mod3, ew.T, seq)
    return xc.reshape(batch, seq, d)
```

```python
import jax
import jax.numpy as jnp
from jax import lax
from jax.experimental import pallas as pl
from jax.experimental.pallas import tpu as pltpu

F32 = jnp.float32
BF16 = jnp.bfloat16

D_MODEL = 2048
D_ATTN = 1024
D_SSM = 1024
HEAD_DIM = 128
N_ATTN_HEADS = 8
ROT_DIM = 32
ROPE_THETA = 500000.0
ATTN_BLOCK = 128
DILATIONS = (1, 4, 16)
SSM_HEAD_DIM = 64
N_SSM_HEADS = 16
SSM_GROUPS = 8
SSM_STATE = 128
CONV_WIDTH = 4
SSD_CHUNK = 128
N_MAIN = 3 * D_ATTN + D_SSM + D_SSM + 2 * SSM_GROUPS * SSM_STATE
N_EXPERT_GROUPS = 8
EXPERTS_PER_GROUP = 8
N_EXPERTS = 64
D_EXPERT = 512
EPS = 1e-6

LANES = 128
SUBLANES = 8
VMEM_LIMIT = 56 * 1024 * 1024

ADALN_TN = 1024
INPROJ_TM = 1024
INPROJ_TN = 1024
ATTN_GROUP = 5
OUT_TM = 512
OUT_SUB = 128
MOE_ROWS = 256


def _cparams(sem):
    return pltpu.CompilerParams(dimension_semantics=sem, vmem_limit_bytes=VMEM_LIMIT)


def _pack_bf16_pair(x):
    k = x.shape[1] // 2
    lo = lax.bitcast_convert_type(x[:, :k].astype(BF16).astype(F32), jnp.uint32)
    hi = lax.bitcast_convert_type(x[:, k:].astype(BF16).astype(F32), jnp.uint32)
    return (lo >> 16) | (hi & jnp.uint32(0xFFFF0000))


def _unpack_bf16_pair(p):
    lo = lax.bitcast_convert_type(p << 16, F32)
    hi = lax.bitcast_convert_type(p & jnp.uint32(0xFFFF0000), F32)
    return lo, hi


def _adaln_kernel(c_ref, w_ref, b_ref, o_ref):
    c = c_ref[...]
    ca = (c * jax.nn.sigmoid(c)).astype(BF16)
    o_ref[...] = jnp.dot(ca, w_ref[...].astype(BF16), preferred_element_type=F32) + b_ref[...]


def _adaln(c, ada_w, ada_b):
    b, d = c.shape
    n = ada_w.shape[1]
    return pl.pallas_call(
        _adaln_kernel,
        grid=(n // ADALN_TN,),
        in_specs=[pl.BlockSpec((b, d), lambda j: (0, 0)),
                  pl.BlockSpec((d, ADALN_TN), lambda j: (0, j)),
                  pl.BlockSpec((1, ADALN_TN), lambda j: (0, j))],
        out_specs=pl.BlockSpec((b, ADALN_TN), lambda j: (0, j)),
        out_shape=jax.ShapeDtypeStruct((b, n), F32),
        compiler_params=_cparams(("arbitrary",)),
        name="adaln",
    )(c, ada_w, ada_b.reshape(1, n))


def _modulated_norm(x, nw, shift, scale):
    y = x * lax.rsqrt(jnp.mean(x * x, axis=-1, keepdims=True) + EPS) * nw
    return y * (1.0 + scale) + shift


def _inproj_kernel(x_ref, mod_ref, nw_ref, w_ref, wdt_ref, o_ref, dt_ref, h_scr):
    @pl.when(pl.program_id(1) == 0)
    def _():
        h = _modulated_norm(x_ref[...], nw_ref[...], mod_ref[0, 0:1, :], mod_ref[0, 1:2, :])
        h_scr[...] = h.astype(BF16)
        dt_ref[...] = jnp.dot(h_scr[...], wdt_ref[...], preferred_element_type=F32)

    o_ref[...] = jnp.dot(h_scr[...], w_ref[...], preferred_element_type=F32).astype(BF16)


def _in_proj(x2, mod3, norm_w, w_main, w_dt, seq):
    t, d = x2.shape
    tm, tn = INPROJ_TM, INPROJ_TN
    return pl.pallas_call(
        _inproj_kernel,
        grid=(t // tm, N_MAIN // tn),
        in_specs=[pl.BlockSpec((tm, d), lambda i, j: (i, 0)),
                  pl.BlockSpec((1, 6, d), lambda i, j: (i * tm // seq, 0, 0)),
                  pl.BlockSpec((1, d), lambda i, j: (0, 0)),
                  pl.BlockSpec((d, tn), lambda i, j: (0, j)),
                  pl.BlockSpec((d, LANES), lambda i, j: (0, 0))],
        out_specs=[pl.BlockSpec((tm, tn), lambda i, j: (i, j)),
                   pl.BlockSpec((tm, LANES), lambda i, j: (i, 0))],
        out_shape=[jax.ShapeDtypeStruct((t, N_MAIN), BF16),
                   jax.ShapeDtypeStruct((t, LANES), F32)],
        scratch_shapes=[pltpu.VMEM((tm, d), BF16)],
        compiler_params=_cparams(("parallel", "arbitrary")),
        name="in_proj",
    )(x2, mod3, norm_w, w_main, w_dt)


def _attn_kernel(q_ref, k_ref, v_ref, pos_ref, freq_ref, qw_ref, kw_ref, o_ref,
                 cos_s, sa_s, qs, ks, vs, acc_s, m_s, l_s, tmp):
    seq = q_ref.shape[0]
    blk = ATTN_BLOCK
    half = ROT_DIM // 2

    @pl.when(pl.program_id(1) == 0)
    def _():
        ang = pos_ref[0].astype(F32) * freq_ref[...]
        lane = lax.broadcasted_iota(jnp.int32, (seq, LANES), 1)
        sn = jnp.sin(ang)
        cos_s[...] = jnp.cos(ang)
        sa_s[...] = jnp.where(lane < half, -sn, jnp.where(lane < ROT_DIM, sn, 0.0))

    sj = lax.broadcasted_iota(jnp.int32, (LANES, LANES), 0)
    si = lax.broadcasted_iota(jnp.int32, (LANES, LANES), 1)
    swap = (((si < half) & (sj == si + half)) | ((si >= half) & (si < ROT_DIM) & (sj == si - half))).astype(BF16)

    def norm_rope(t_ref, w_ref):
        t = t_ref[...].astype(F32)
        y = t * lax.rsqrt(jnp.mean(t * t, axis=-1, keepdims=True) + EPS) * w_ref[...]
        y_hi = y.astype(BF16)
        y_lo = (y - y_hi.astype(F32)).astype(BF16)
        partner = (jnp.dot(y_hi, swap, preferred_element_type=F32)
                   + jnp.dot(y_lo, swap, preferred_element_type=F32))
        return y * cos_s[...] + partner * sa_s[...]

    qseg = seq // 4
    slab = blk // 4

    def to_quarter(val, dst):
        tmp[...] = val
        for r in range(4):
            dst[r * qseg:(r + 1) * qseg, :] = tmp[pl.ds(r, qseg, stride=4), :]

    to_quarter(norm_rope(q_ref, qw_ref), qs)
    to_quarter(norm_rope(k_ref, kw_ref), ks)
    to_quarter(v_ref[...].astype(F32), vs)

    qi = lax.broadcasted_iota(jnp.int32, (blk, blk), 0)
    kj = lax.broadcasted_iota(jnp.int32, (blk, blk), 1)
    pq = 4 * (qi % slab) + qi // slab
    pk = 4 * (kj % slab) + kj // slab
    masks = {
        "seq": (kj <= qi, jnp.concatenate([kj >= qi, kj <= qi], axis=1)),
        "slab": (pk <= pq, jnp.concatenate([pk >= pq, pk <= pq], axis=1)),
    }
    ones_blk = jnp.ones((blk, LANES), BF16)
    scale = HEAD_DIM ** -0.5
    nt = (((1,), (1,)), ((), ()))

    def load(ref, kind, start):
        if kind == "c":
            return ref[pl.ds(start, blk), :]
        if kind == "s":
            return ref[pl.ds(start, blk, stride=4), :]
        return jnp.concatenate([ref[pl.ds(start + r * qseg, slab), :] for r in range(4)], axis=0)

    def store(ref, kind, start, val):
        if kind == "c":
            ref[pl.ds(start, blk), :] = val
        elif kind == "s":
            ref[pl.ds(start, blk, stride=4), :] = val
        else:
            for r in range(4):
                ref[pl.ds(start + r * qseg, slab), :] = val[r * slab:(r + 1) * slab]

    def group(kind, starts, with_prev, first_pattern):
        step = slab if kind == "q" else blk
        qb = [load(qs, kind, s).astype(BF16) for s in starts]
        kb = [load(ks, kind, s).astype(BF16) for s in starts]
        vb = [jnp.concatenate([load(vs, kind, s).astype(BF16), ones_blk], axis=1) for s in starts]
        if with_prev:
            kb = [jnp.concatenate([load(ks, kind, s - step).astype(BF16), k], axis=0)
                  for s, k in zip(starts, kb)]
            vb = [jnp.concatenate([jnp.concatenate([load(vs, kind, s - step).astype(BF16), ones_blk], axis=1),
                                   v], axis=0) for s, v in zip(starts, vb)]
        mask = masks["slab" if kind == "q" else "seq"][1 if with_prev else 0]
        sc = [jnp.where(mask, lax.dot_general(q, k, nt, preferred_element_type=F32) * scale, -jnp.inf)
              for q, k in zip(qb, kb)]
        if with_prev:
            mx = [jnp.max(jnp.maximum(x[:, :blk], x[:, blk:]), axis=-1, keepdims=True) for x in sc]
        else:
            mx = [jnp.max(x, axis=-1, keepdims=True) for x in sc]
        pr = [jnp.exp(x - m).astype(BF16) for x, m in zip(sc, mx)]
        ov = [jnp.dot(p, v, preferred_element_type=F32) for p, v in zip(pr, vb)]
        mb = [jnp.broadcast_to(m, (blk, LANES)) for m in mx]
        if first_pattern:
            new = [(m, o[:, LANES:], o[:, :LANES]) for m, o in zip(mb, ov)]
        else:
            old = [(load(m_s, kind, s), load(l_s, kind, s), load(acc_s, kind, s)) for s in starts]
            new = []
            for (m_old, l_old, acc_old), m, o in zip(old, mb, ov):
                m_new = jnp.maximum(m_old, m)
                a_old = jnp.exp(m_old - m_new)
                a_new = jnp.exp(m - m_new)
                new.append((m_new, a_old * l_old + a_new * o[:, LANES:],
                            a_old * acc_old + a_new * o[:, :LANES]))
        for s, (m_new, l_new, acc_new) in zip(starts, new):
            store(m_s, kind, s, m_new)
            store(l_s, kind, s, l_new)
            store(acc_s, kind, s, acc_new)

    def loop(n_iter, kind, starts_of, with_prev, first_pattern):
        def body(i, carry):
            group(kind, starts_of(i), with_prev, first_pattern)
            return carry
        lax.fori_loop(0, n_iter, body, 0)

    nb4 = qseg // blk
    group("c", [r * qseg for r in range(4)], False, True)
    loop(2, "c", lambda i: [(2 * i + j) * qseg + n * blk for j in range(2) for n in range(1, nb4)], True, True)
    loop(4, "s", lambda r: [r * qseg + a for a in range(4)], False, False)
    group("q", [0], False, False)
    loop((seq // blk - 1) // ATTN_GROUP, "q",
         lambda i: [(1 + ATTN_GROUP * i + j) * slab for j in range(ATTN_GROUP)], True, False)

    tmp[...] = acc_s[...] / l_s[...]
    for r in range(4):
        qs[pl.ds(r, qseg, stride=4), :] = tmp[r * qseg:(r + 1) * qseg, :]
    o_ref[...] = qs[...].astype(BF16)


def _attention(proj, pos3, freq, qw, kw, batch, seq):
    t = proj.shape[0]
    hd = HEAD_DIM
    nh = N_ATTN_HEADS
    assert seq % (ATTN_BLOCK * max(DILATIONS)) == 0 and (seq // ATTN_BLOCK - 1) % ATTN_GROUP == 0
    scr = pltpu.VMEM((seq, hd), F32)
    return pl.pallas_call(
        _attn_kernel,
        grid=(batch, nh),
        in_specs=[pl.BlockSpec((seq, hd), lambda b, h: (b, h)),
                  pl.BlockSpec((seq, hd), lambda b, h: (b, nh + h)),
                  pl.BlockSpec((seq, hd), lambda b, h: (b, 2 * nh + h)),
                  pl.BlockSpec((1, seq, 1), lambda b, h: (b, 0, 0)),
                  pl.BlockSpec((1, hd), lambda b, h: (0, 0)),
                  pl.BlockSpec((1, hd), lambda b, h: (0, 0)),
                  pl.BlockSpec((1, hd), lambda b, h: (0, 0))],
        out_specs=pl.BlockSpec((seq, hd), lambda b, h: (b, h)),
        out_shape=jax.ShapeDtypeStruct((t, D_ATTN), BF16),
        scratch_shapes=[scr] * 9,
        compiler_params=_cparams(("parallel", "arbitrary")),
        name="attention",
    )(proj, proj, proj, pos3, freq, qw, kw)


def _split3(v):
    hi = v.astype(BF16)
    r1 = v - hi.astype(F32)
    mid = r1.astype(BF16)
    lo = (r1 - mid.astype(F32)).astype(BF16)
    return hi, mid, lo


def _ssd_kernel(xs_ref, bm_ref, cm_ref, z_ref, dtr_ref, cwx_ref, cwb_ref, cwc_ref,
                cbx_ref, cbb_ref, cbc_ref, dtb_ref, alog_ref, dsk_ref, nw_ref, o_ref,
                xpad, x_s, b_s, c_s, dt_s, da_s, dtb_s, daf_s):
    seq = xs_ref.shape[0]
    ck = SSD_CHUNK
    g = pl.program_id(1)
    pad = SUBLANES

    def conv_silu(src_ref, w_ref, b_ref, dst):
        xpad[0:pad, :] = jnp.zeros((pad, LANES), F32)
        xpad[pad:, :] = src_ref[...].astype(F32)
        y = b_ref[...] + jnp.zeros((seq, LANES), F32)
        for j in range(CONV_WIDTH):
            y = y + w_ref[j:j + 1, :] * xpad[pl.ds(pad - (CONV_WIDTH - 1) + j, seq), :]
        dst[...] = y * jax.nn.sigmoid(y)

    conv_silu(xs_ref, cwx_ref, cbx_ref, x_s)
    conv_silu(bm_ref, cwb_ref, cbb_ref, b_s)
    conv_silu(cm_ref, cwc_ref, cbc_ref, c_s)

    @pl.when(g == 0)
    def _():
        xr = dtr_ref[...] + dtb_ref[...]
        dt = jnp.maximum(xr, 0.0) + jnp.log1p(jnp.exp(-jnp.abs(xr)))
        for i, piece in enumerate(_split3(dt)):
            dt_s[i] = piece
        for i, piece in enumerate(_split3(dt * (-jnp.exp(alog_ref[...])))):
            da_s[i] = piece

    def select(v_s, i, sel):
        return jnp.dot(v_s[i], sel, preferred_element_type=F32)

    hrow = lax.broadcasted_iota(jnp.int32, (LANES, 2 * LANES), 0)
    hlane = lax.broadcasted_iota(jnp.int32, (LANES, 2 * LANES), 1)
    brow = lax.broadcasted_iota(jnp.int32, (LANES, LANES), 0)
    blane = lax.broadcasted_iota(jnp.int32, (LANES, LANES), 1)
    sel_b = (brow == 2 * g + (blane >= SSM_HEAD_DIM).astype(jnp.int32)).astype(BF16)
    sel_f = (hrow == 2 * g + (hlane >= LANES).astype(jnp.int32)).astype(BF16)
    dtb_s[...] = select(dt_s, 0, sel_b) + select(dt_s, 1, sel_b) + select(dt_s, 2, sel_b)
    for i in range(3):
        daf_s[i] = select(da_s, i, sel_f).astype(BF16)

    li = lax.broadcasted_iota(jnp.int32, (ck, ck), 0)
    si = lax.broadcasted_iota(jnp.int32, (ck, ck), 1)
    causal = si <= li
    eye = si == li
    tril_bf = causal.astype(BF16)
    head1 = lax.broadcasted_iota(jnp.int32, (ck, LANES), 1) >= SSM_HEAD_DIM
    head1_row = lax.broadcasted_iota(jnp.int32, (1, LANES), 1) >= SSM_HEAD_DIM

    def chunk(c, prev):
        r = pl.ds(pl.multiple_of(c * ck, ck), ck)
        xc = x_s[r, :]
        bc = b_s[r, :].astype(BF16)
        cc = c_s[r, :].astype(BF16)
        acs2 = (jnp.dot(tril_bf, daf_s[0, r, :], preferred_element_type=F32)
                + jnp.dot(tril_bf, daf_s[1, r, :], preferred_element_type=F32)
                + jnp.dot(tril_bf, daf_s[2, r, :], preferred_element_type=F32))
        acs_f = (acs2[:, :LANES], acs2[:, LANES:])
        acs_b = jnp.where(head1, acs_f[1], acs_f[0])
        last_b = jnp.where(head1_row, acs_f[1][ck - 1:ck, :], acs_f[0][ck - 1:ck, :])
        cb = lax.dot_general(cc, bc, (((1,), (1,)), ((), ())), preferred_element_type=F32)
        xdt = xc * dtb_s[r, :]
        scores = []
        for k in range(2):
            acs_row = jnp.sum(jnp.where(eye, acs_f[k], 0.0), axis=0, keepdims=True)
            decay = jnp.exp(jnp.where(causal, acs_f[k] - acs_row, -jnp.inf))
            scores.append((cb * decay).astype(BF16))
        xdt2 = jnp.concatenate([jnp.where(head1, 0.0, xdt).astype(BF16),
                                jnp.where(head1, xdt, 0.0).astype(BF16)], axis=0)
        y_diag = jnp.dot(jnp.concatenate(scores, axis=1), xdt2, preferred_element_type=F32)
        y_off = jnp.dot(cc, prev.astype(BF16), preferred_element_type=F32) * jnp.exp(acs_b)
        y = y_diag + y_off + dsk_ref[...] * xc
        zc = z_ref[r, :].astype(F32)
        y = y * (zc * jax.nn.sigmoid(zc))
        y = y * lax.rsqrt(jnp.mean(y * y, axis=-1, keepdims=True) + EPS) * nw_ref[...]
        o_ref[r, :] = y.astype(BF16)
        w = (xdt * jnp.exp(last_b - acs_b)).astype(BF16)
        st = lax.dot_general(bc, w, (((0,), (0,)), ((), ())), preferred_element_type=F32)
        return prev * jnp.exp(last_b) + st

    lax.fori_loop(0, seq // ck, chunk, jnp.zeros((SSM_STATE, LANES), F32), unroll=8)


def _ssd(proj, dt_raw, conv_w, conv_b, dtb, alog, dsk, nw, batch, seq):
    t = proj.shape[0]
    gs = SSM_GROUPS
    col = lambda off: (lambda b, g: (b, off + g))
    wcol = lambda off: (lambda b, g: (0, off + g))
    zq, xq, bq, cq = 3 * D_ATTN // LANES, 4 * D_ATTN // LANES, 5 * D_ATTN // LANES, 6 * D_ATTN // LANES
    const = lambda b, g: (0, 0)
    scr = pltpu.VMEM((seq, LANES), F32)
    return pl.pallas_call(
        _ssd_kernel,
        grid=(batch, gs),
        in_specs=[pl.BlockSpec((seq, LANES), col(xq)),
                  pl.BlockSpec((seq, LANES), col(bq)),
                  pl.BlockSpec((seq, LANES), col(cq)),
                  pl.BlockSpec((seq, LANES), col(zq)),
                  pl.BlockSpec((seq, LANES), lambda b, g: (b, 0)),
                  pl.BlockSpec((CONV_WIDTH, LANES), wcol(0)),
                  pl.BlockSpec((CONV_WIDTH, LANES), wcol(gs)),
                  pl.BlockSpec((CONV_WIDTH, LANES), wcol(2 * gs)),
                  pl.BlockSpec((1, LANES), wcol(0)),
                  pl.BlockSpec((1, LANES), wcol(gs)),
                  pl.BlockSpec((1, LANES), wcol(2 * gs)),
                  pl.BlockSpec((1, LANES), const),
                  pl.BlockSpec((1, LANES), const),
                  pl.BlockSpec((1, LANES), lambda b, g: (0, g)),
                  pl.BlockSpec((1, LANES), lambda b, g: (0, g))],
        out_specs=pl.BlockSpec((seq, LANES), lambda b, g: (b, g)),
        out_shape=jax.ShapeDtypeStruct((t, D_SSM), BF16),
        scratch_shapes=[pltpu.VMEM((seq + SUBLANES, LANES), F32), scr, scr, scr,
                        pltpu.VMEM((3, seq, LANES), BF16), pltpu.VMEM((3, seq, LANES), BF16), scr,
                        pltpu.VMEM((3, seq, 2 * LANES), BF16)],
        compiler_params=_cparams(("arbitrary", "arbitrary")),
        name="ssd",
    )(proj, proj, proj, proj, dt_raw, conv_w, conv_w, conv_w, conv_b, conv_b, conv_b,
      dtb, alog, dsk, nw)


def _out_router_kernel(ya_ref, ys_ref, wa_ref, ws_ref, x_ref, mod_ref, nw_ref, rw_ref, rb_ref,
                       x1_ref, h2_ref, eid_ref, ew_ref, rank_ref, cnt_ref, cnt_s):
    tm = x_ref.shape[0]
    parts = []
    for r0 in range(0, tm, OUT_SUB):
        rows = slice(r0, r0 + OUT_SUB)
        mix = (jnp.dot(ya_ref[rows, :], wa_ref[...], preferred_element_type=F32)
               + jnp.dot(ys_ref[rows, :], ws_ref[...], preferred_element_type=F32))
        x1 = x_ref[rows, :] + mod_ref[0, 2:3, :] * mix
        x1_ref[rows, :] = x1
        part = _modulated_norm(x1, nw_ref[...], mod_ref[0, 3:4, :], mod_ref[0, 4:5, :])
        h2_ref[rows, :] = _pack_bf16_pair(part)
        parts.append(part)
    h2 = jnp.concatenate(parts, axis=0)

    h_hi = h2.astype(BF16)
    h_lo = (h2 - h_hi.astype(F32)).astype(BF16)
    rw = rw_ref[...]
    w_hi = rw.astype(BF16)
    w_lo = (rw - w_hi.astype(F32)).astype(BF16)
    nt = (((1,), (1,)), ((), ()))
    logits = (lax.dot_general(w_hi, h_hi, nt, preferred_element_type=F32)
              + lax.dot_general(w_hi, h_lo, nt, preferred_element_type=F32)
              + lax.dot_general(w_lo, h_hi, nt, preferred_element_type=F32)) + rb_ref[...]

    ng, ne = N_EXPERT_GROUPS, EXPERTS_PER_GROUP
    iota = lax.broadcasted_iota(jnp.int32, (ne, tm), 0)

    def argmax0(v):
        mx = jnp.max(v, axis=0, keepdims=True)
        idx = jnp.min(jnp.where(v == mx, iota, ne), axis=0, keepdims=True)
        return mx, idx

    gl = logits[N_EXPERTS:N_EXPERTS + ng, :]
    gmax, gidx = argmax0(gl)
    g_gate = 1.0 / jnp.sum(jnp.exp(gl - gmax), axis=0, keepdims=True)
    e_in = jnp.zeros((ne, tm), F32)
    for gi in range(ng):
        e_in = e_in + jnp.where(gidx == gi, logits[gi * ne:(gi + 1) * ne, :], 0.0)
    m1, i1 = argmax0(e_in)
    m2, i2 = argmax0(jnp.where(iota == i1, -jnp.inf, e_in))
    p2 = jnp.exp(m2 - m1)
    den = 1.0 + p2
    e0 = gidx * ne + i1
    e1 = gidx * ne + i2
    eid_ref[0, 0:1, :] = e0
    eid_ref[0, 1:2, :] = e1
    ew_ref[0:1, :] = (1.0 / den) * g_gate
    ew_ref[1:2, :] = (p2 / den) * g_gate

    @pl.when(pl.program_id(0) == 0)
    def _():
        cnt_s[...] = jnp.zeros(cnt_s.shape, F32)

    eiota = lax.broadcasted_iota(jnp.int32, (N_EXPERTS, tm), 0)
    oh0 = eiota == e0
    oh1 = eiota == e1
    both = jnp.logical_or(oh0, oh1)
    ti = lax.broadcasted_iota(jnp.int32, (tm, tm), 0)
    tj = lax.broadcasted_iota(jnp.int32, (tm, tm), 1)
    earlier = (ti < tj).astype(BF16)
    before = cnt_s[...] + jnp.dot(both.astype(BF16), earlier, preferred_element_type=F32)
    rank_ref[0, 0:1, :] = jnp.sum(jnp.where(oh0, before, 0.0), axis=0, keepdims=True).astype(jnp.int32)
    rank_ref[0, 1:2, :] = jnp.sum(jnp.where(oh1, before, 0.0), axis=0, keepdims=True).astype(jnp.int32)
    total = cnt_s[...] + jnp.sum(both.astype(F32), axis=1, keepdims=True)
    cnt_s[...] = total
    cnt_ref[...] = total.astype(jnp.int32)


def _out_router(ya, ys, wa, ws, x2, mod3, norm_w, rw, rb, seq):
    t, d = x2.shape
    tm = OUT_TM
    const = lambda i: (0, 0)
    return pl.pallas_call(
        _out_router_kernel,
        grid=(t // tm,),
        in_specs=[pl.BlockSpec((tm, D_ATTN), lambda i: (i, 0)),
                  pl.BlockSpec((tm, D_SSM), lambda i: (i, 0)),
                  pl.BlockSpec((D_ATTN, d), const, pipeline_mode=pl.Buffered(1)),
                  pl.BlockSpec((D_SSM, d), const, pipeline_mode=pl.Buffered(1)),
                  pl.BlockSpec((tm, d), lambda i: (i, 0)),
                  pl.BlockSpec((1, 6, d), lambda i: (i * tm // seq, 0, 0)),
                  pl.BlockSpec((1, d), const),
                  pl.BlockSpec((LANES, d), const, pipeline_mode=pl.Buffered(1)),
                  pl.BlockSpec((LANES, 1), const)],
        out_specs=[pl.BlockSpec((tm, d), lambda i: (i, 0)),
                   pl.BlockSpec((tm, d // 2), lambda i: (i, 0)),
                   pl.BlockSpec((1, 2, tm), lambda i: (i, 0, 0)),
                   pl.BlockSpec((2, tm), lambda i: (0, i)),
                   pl.BlockSpec((1, 2, tm), lambda i: (i, 0, 0)),
                   pl.BlockSpec((N_EXPERTS, 1), const)],
        out_shape=[jax.ShapeDtypeStruct((t, d), F32),
                   jax.ShapeDtypeStruct((t, d // 2), jnp.uint32),
                   jax.ShapeDtypeStruct((t // tm, 2, tm), jnp.int32),
                   jax.ShapeDtypeStruct((2, t), F32),
                   jax.ShapeDtypeStruct((t // tm, 2, tm), jnp.int32),
                   jax.ShapeDtypeStruct((N_EXPERTS, 1), jnp.int32)],
        scratch_shapes=[pltpu.VMEM((N_EXPERTS, 1), F32)],
        compiler_params=_cparams(("arbitrary",)),
        name="out_router",
    )(ya, ys, wa, ws, x2, mod3, norm_w, rw, rb)


def _for_each_assignment(tm, body):
    def group(gi, carry):
        for jj in range(SUBLANES):
            for k in range(2):
                body(k, gi, jj, k * tm + gi * SUBLANES + jj)
        return carry
    lax.fori_loop(0, tm // SUBLANES, group, 0)


def _rows3(a):
    return a.reshape(a.shape[0] // SUBLANES, SUBLANES, a.shape[1])


def _scatter_kernel(zstart_ref, dest_ref, h_ref, xd_hbm, sem, zbuf, zsem):
    tm = h_ref.shape[0] * SUBLANES

    @pl.when(pl.program_id(0) == 0)
    def _():
        zbuf[...] = jnp.zeros(zbuf.shape, zbuf.dtype)

        def zero_copy(j):
            start = pl.multiple_of(jnp.maximum(zstart_ref[j], 0), MOE_ROWS)
            return pltpu.make_async_copy(zbuf, xd_hbm.at[pl.ds(start, MOE_ROWS)], zsem)

        def z_issue(j, carry):
            @pl.when(zstart_ref[j] >= 0)
            def _():
                zero_copy(j).start()
            return carry

        def z_wait(j, carry):
            @pl.when(zstart_ref[j] >= 0)
            def _():
                zero_copy(j).wait()
            return carry

        lax.fori_loop(0, zstart_ref.shape[0], z_issue, 0)
        lax.fori_loop(0, zstart_ref.shape[0], z_wait, 0)

    def issue(k, gi, jj, j):
        pltpu.make_async_copy(h_ref.at[gi, pl.ds(jj, 1)], xd_hbm.at[pl.ds(dest_ref[0, 0, j], 1)], sem).start()

    _for_each_assignment(tm, issue)
    for _ in range(2 * SUBLANES):
        pltpu.make_async_copy(h_ref.at[:, 0], xd_hbm.at[pl.ds(0, tm // SUBLANES)], sem).wait()


def _scatter_rows(zstart, dest3, h2, n_rows):
    t, d = h2.shape
    n_tiles = dest3.shape[0]
    tm = dest3.shape[2] // 2
    grid_spec = pltpu.PrefetchScalarGridSpec(
        num_scalar_prefetch=1,
        grid=(n_tiles,),
        in_specs=[pl.BlockSpec((1, 1, 2 * tm), lambda i, zs: (i, 0, 0), memory_space=pltpu.SMEM),
                  pl.BlockSpec((tm // SUBLANES, SUBLANES, d), lambda i, zs: (i, 0, 0))],
        out_specs=pl.BlockSpec(memory_space=pl.ANY),
        scratch_shapes=[pltpu.SemaphoreType.DMA(()), pltpu.VMEM((MOE_ROWS, d), h2.dtype),
                        pltpu.SemaphoreType.DMA(())],
    )
    return pl.pallas_call(
        _scatter_kernel,
        grid_spec=grid_spec,
        out_shape=jax.ShapeDtypeStruct((n_rows, d), h2.dtype),
        compiler_params=_cparams(("arbitrary",)),
        name="scatter_rows",
    )(zstart, dest3, _rows3(h2))


def _experts_kernel(be_ref, nv_ref, first_ref, nxt_ref, slot_ref, x_ref, wg_hbm, wu_hbm, wd_hbm, o_ref,
                    wg_f, wu_f, wd_f, sems, wg_s, wu_s, wd_s):
    i = pl.program_id(0)
    n_valid = nv_ref[i]
    half = x_ref.shape[1]

    def weight_copies(e, slot):
        return (pltpu.make_async_copy(wg_hbm.at[e], wg_f.at[slot], sems.at[slot, 0]),
                pltpu.make_async_copy(wu_hbm.at[e], wu_f.at[slot], sems.at[slot, 1]),
                pltpu.make_async_copy(wd_hbm.at[e], wd_f.at[slot], sems.at[slot, 2]))

    @pl.when(i == 0)
    def _():
        for c in weight_copies(be_ref[0], slot_ref[0]):
            c.start()

    @pl.when(first_ref[i] == 1)
    def _():
        slot = slot_ref[i]
        for c in weight_copies(be_ref[i], slot):
            c.wait()

        @pl.when(nxt_ref[i] >= 0)
        def _():
            for c in weight_copies(nxt_ref[i], 1 - slot):
                c.start()

        wg_s[...] = wg_f[slot].astype(BF16)
        wu_s[...] = wu_f[slot].astype(BF16)
        wd_s[...] = wd_f[slot].astype(BF16)

    @pl.when(n_valid > 0)
    def _():
        x_lo, x_hi = _unpack_bf16_pair(x_ref[...])
        x_lo = x_lo.astype(BF16)
        x_hi = x_hi.astype(BF16)
        gt = (jnp.dot(x_lo, wg_s[0:half, :], preferred_element_type=F32)
              + jnp.dot(x_hi, wg_s[half:, :], preferred_element_type=F32))
        up = (jnp.dot(x_lo, wu_s[0:half, :], preferred_element_type=F32)
              + jnp.dot(x_hi, wu_s[half:, :], preferred_element_type=F32))
        hid = (gt * jax.nn.sigmoid(gt)) * up
        o_ref[...] = _pack_bf16_pair(jnp.dot(hid.astype(BF16), wd_s[...], preferred_element_type=F32))

    @pl.when(n_valid == 0)
    def _():
        o_ref[...] = jnp.zeros(o_ref.shape, o_ref.dtype)


def _experts(block_e, n_valid, first, nxt, slot, xd, w_gate, w_up, w_down):
    rows = MOE_ROWS
    n_blocks = xd.shape[0] // rows
    half = xd.shape[1]
    d = 2 * half
    de = w_gate.shape[2]
    blk = lambda i, *_: (i, 0)
    grid_spec = pltpu.PrefetchScalarGridSpec(
        num_scalar_prefetch=5,
        grid=(n_blocks,),
        in_specs=[pl.BlockSpec((rows, half), blk),
                  pl.BlockSpec(memory_space=pl.ANY),
                  pl.BlockSpec(memory_space=pl.ANY),
                  pl.BlockSpec(memory_space=pl.ANY)],
        out_specs=pl.BlockSpec((rows, half), blk),
        scratch_shapes=[pltpu.VMEM((2, d, de), F32),
                        pltpu.VMEM((2, d, de), F32),
                        pltpu.VMEM((2, de, d), F32),
                        pltpu.SemaphoreType.DMA((2, 3)),
                        pltpu.VMEM((d, de), BF16),
                        pltpu.VMEM((d, de), BF16),
                        pltpu.VMEM((de, d), BF16)],
    )
    return pl.pallas_call(
        _experts_kernel,
        grid_spec=grid_spec,
        out_shape=jax.ShapeDtypeStruct((n_blocks * rows, half), jnp.uint32),
        compiler_params=_cparams(("arbitrary",)),
        name="experts",
    )(block_e, n_valid, first, nxt, slot, xd, w_gate, w_up, w_down)


def _combine_kernel(dest_ref, destn_ref, yd_hbm, x1_ref, mod_ref, ew_ref, o_ref, ybuf, sems):
    i = pl.program_id(0)
    n = pl.num_programs(0)
    tm = x1_ref.shape[0]
    tg = tm // SUBLANES
    slot = i % 2

    def gather(d_ref, s):
        def issue(k, gi, jj, j):
            pltpu.make_async_copy(yd_hbm.at[pl.ds(d_ref[0, 0, j], 1)],
                                  ybuf.at[s, k * tg + gi, pl.ds(jj, 1)], sems.at[s]).start()
        _for_each_assignment(tm, issue)

    @pl.when(i == 0)
    def _():
        gather(dest_ref, 0)

    @pl.when(i + 1 < n)
    def _():
        gather(destn_ref, 1 - slot)

    for _ in range(SUBLANES):
        pltpu.make_async_copy(yd_hbm.at[pl.ds(0, 2 * tg)], ybuf.at[slot, :, 0], sems.at[slot]).wait()
    half = ybuf.shape[3]
    y0_lo, y0_hi = _unpack_bf16_pair(ybuf[slot, 0:tg].reshape(tm, half))
    y1_lo, y1_hi = _unpack_bf16_pair(ybuf[slot, tg:2 * tg].reshape(tm, half))
    w0 = ew_ref[:, 0:1]
    w1 = ew_ref[:, 1:2]
    o_ref[:, 0:half] = x1_ref[:, 0:half] + mod_ref[0, 5:6, 0:half] * (w0 * y0_lo + w1 * y1_lo)
    o_ref[:, half:] = x1_ref[:, half:] + mod_ref[0, 5:6, half:] * (w0 * y0_hi + w1 * y1_hi)


def _combine(dest3, yd, x1, mod3, ew_t, seq):
    t, d = x1.shape
    n = dest3.shape[0]
    tm = dest3.shape[2] // 2
    smem = lambda im: pl.BlockSpec((1, 1, 2 * tm), im, memory_space=pltpu.SMEM)
    return pl.pallas_call(
        _combine_kernel,
        grid=(n,),
        in_specs=[smem(lambda i: (i, 0, 0)), smem(lambda i: (jnp.minimum(i + 1, n - 1), 0, 0)),
                  pl.BlockSpec(memory_space=pl.ANY),
                  pl.BlockSpec((tm, d), lambda i: (i, 0)),
                  pl.BlockSpec((1, 6, d), lambda i: (i * tm // seq, 0, 0)),
                  pl.BlockSpec((tm, 2), lambda i: (i, 0))],
        out_specs=pl.BlockSpec((tm, d), lambda i: (i, 0)),
        out_shape=jax.ShapeDtypeStruct((t, d), F32),
        scratch_shapes=[pltpu.VMEM((2, 2 * tm // SUBLANES, SUBLANES, yd.shape[1]), yd.dtype),
                        pltpu.SemaphoreType.DMA((2,))],
        compiler_params=_cparams(("arbitrary",)),
        name="combine",
    )(dest3, dest3, yd, x1, mod3, ew_t)


def _layout(counts, n_assign):
    rows = MOE_ROWS
    n_blocks = (n_assign + N_EXPERTS * (rows - 1)) // rows
    padded = (counts + rows - 1) // rows * rows
    pend = jnp.cumsum(padded)
    pstart = pend - padded
    bstart = jnp.arange(n_blocks, dtype=jnp.int32) * rows
    block_e = jnp.minimum(jnp.sum((pend[None, :] <= bstart[:, None]).astype(jnp.int32), axis=1),
                          N_EXPERTS - 1)
    n_valid = jnp.clip(pstart[block_e] + counts[block_e] - bstart, 0, rows)
    n_valid = jnp.where(bstart < pend[-1], n_valid, 0)
    min_used = n_assign // rows
    tail = pend[-1] + jnp.arange(n_blocks - min_used, dtype=jnp.int32) * rows
    zstart = jnp.concatenate([jnp.where(counts > 0, pend - rows, -1),
                              jnp.where(tail < n_blocks * rows, tail, -1)])
    first = (n_valid > 0) & jnp.concatenate([jnp.ones((1,), bool), block_e[1:] != block_e[:-1]])
    slot = (jnp.cumsum(first.astype(jnp.int32)) - 1) % 2
    ids = jnp.where(counts > 0, jnp.arange(N_EXPERTS, dtype=jnp.int32), N_EXPERTS)
    later = jnp.concatenate([lax.cummin(ids[::-1])[::-1][1:], jnp.full((1,), N_EXPERTS, jnp.int32)])
    nxt = jnp.where(later < N_EXPERTS, later, -1)[block_e]
    sched = (first.astype(jnp.int32), nxt.astype(jnp.int32), jnp.maximum(slot, 0).astype(jnp.int32))
    return (pstart.astype(jnp.int32), zstart.astype(jnp.int32), block_e.astype(jnp.int32),
            n_valid.astype(jnp.int32), sched, n_blocks * rows)


def kernel(x, c, positions, ada_w, ada_b, norm1_w, w_in, q_norm_w, k_norm_w, conv_w, conv_b,
           dt_bias, a_log, d_skip, ssm_norm_w, w_out, norm2_w, router_group_w, router_group_b,
           router_expert_w, router_expert_b, w_gate, w_up, w_down):
    batch, seq, d = x.shape
    t = batch * seq
    depth = ada_w.shape[0]
    nh_pad = LANES - N_SSM_HEADS
    inv_freq = ROPE_THETA ** (-(jnp.arange(0, ROT_DIM, 2, dtype=F32) / ROT_DIM))
    freq = jnp.concatenate([inv_freq, inv_freq, jnp.zeros((HEAD_DIM - ROT_DIM,), F32)]).reshape(1, HEAD_DIM)
    pos3 = positions.reshape(batch, seq, 1)
    xc = x.reshape(t, d)
    for l in range(depth):
        mod3 = _adaln(c, ada_w[l], ada_b[l]).reshape(batch, 6, d)
        w_main = w_in[l][:, :N_MAIN].astype(BF16)
        w_dt = jnp.pad(w_in[l][:, N_MAIN:], ((0, 0), (0, nh_pad))).astype(BF16)
        proj, dt_raw = _in_proj(xc, mod3, norm1_w[l].reshape(1, d), w_main, w_dt, seq)
        y_attn = _attention(proj, pos3, freq, q_norm_w[l].reshape(1, HEAD_DIM),
                            k_norm_w[l].reshape(1, HEAD_DIM), batch, seq)
        y_ssm = _ssd(proj, dt_raw, conv_w[l], conv_b[l].reshape(1, -1),
                     jnp.pad(dt_bias[l], (0, nh_pad)).reshape(1, LANES),
                     jnp.pad(a_log[l], (0, nh_pad)).reshape(1, LANES),
                     jnp.repeat(d_skip[l], SSM_HEAD_DIM).reshape(1, D_SSM),
                     ssm_norm_w[l].reshape(1, D_SSM), batch, seq)
        rw = jnp.concatenate([router_expert_w[l].T, router_group_w[l].T,
                              jnp.zeros((LANES - N_EXPERTS - N_EXPERT_GROUPS, d), F32)])
        rb = jnp.concatenate([router_expert_b[l], router_group_b[l],
                              jnp.zeros((LANES - N_EXPERTS - N_EXPERT_GROUPS,), F32)]).reshape(LANES, 1)
        wo = w_out[l].astype(BF16)
        x1, h2, eid3, ew, rank3, counts = _out_router(y_attn, y_ssm, wo[:D_ATTN], wo[D_ATTN:], xc, mod3,
                                                      norm2_w[l].reshape(1, d), rw, rb, seq)
        pstart, zstart, block_e, n_valid, sched, n_rows = _layout(counts.reshape(N_EXPERTS), 2 * t)
        seg = jnp.sum(jnp.where(eid3[..., None] == jnp.arange(N_EXPERTS, dtype=jnp.int32), pstart, 0), axis=-1)
        dest3 = (seg + rank3).reshape(t // OUT_TM, 1, 2 * OUT_TM)
        xd = _scatter_rows(zstart, dest3, h2, n_rows)
        yd = _experts(block_e, n_valid, *sched, xd, w_gate[l], w_up[l], w_down[l])
        xc = _combine(dest3, yd, x1, mod3, ew.T, seq)
    return xc.reshape(batch, seq, d)
```

```python
import jax
import jax.numpy as jnp
from jax import lax
from jax.experimental import pallas as pl
from jax.experimental.pallas import tpu as pltpu

F32 = jnp.float32
BF16 = jnp.bfloat16

D_MODEL = 2048
D_ATTN = 1024
D_SSM = 1024
HEAD_DIM = 128
N_ATTN_HEADS = 8
ROT_DIM = 32
ROPE_THETA = 500000.0
ATTN_BLOCK = 128
DILATIONS = (1, 4, 16)
SSM_HEAD_DIM = 64
N_SSM_HEADS = 16
SSM_GROUPS = 8
SSM_STATE = 128
CONV_WIDTH = 4
SSD_CHUNK = 128
N_MAIN = 3 * D_ATTN + D_SSM + D_SSM + 2 * SSM_GROUPS * SSM_STATE
N_EXPERT_GROUPS = 8
EXPERTS_PER_GROUP = 8
N_EXPERTS = 64
D_EXPERT = 512
EPS = 1e-6

LANES = 128
SUBLANES = 8
VMEM_LIMIT = 56 * 1024 * 1024

ADALN_TN = 1024
INPROJ_TM = 1024
INPROJ_TN = 1024
ATTN_GROUP = 5
OUT_TM = 512
OUT_SUB = 128
MOE_ROWS = 256


def _cparams(sem):
    return pltpu.CompilerParams(dimension_semantics=sem, vmem_limit_bytes=VMEM_LIMIT)


def _pack_bf16_pair(x):
    k = x.shape[1] // 2
    lo = lax.bitcast_convert_type(x[:, :k].astype(BF16).astype(F32), jnp.uint32)
    hi = lax.bitcast_convert_type(x[:, k:].astype(BF16).astype(F32), jnp.uint32)
    return (lo >> 16) | (hi & jnp.uint32(0xFFFF0000))


def _unpack_bf16_pair(p):
    lo = lax.bitcast_convert_type(p << 16, F32)
    hi = lax.bitcast_convert_type(p & jnp.uint32(0xFFFF0000), F32)
    return lo, hi


def _adaln_kernel(c_ref, w_ref, b_ref, o_ref):
    c = c_ref[...]
    ca = (c * jax.nn.sigmoid(c)).astype(BF16)
    o_ref[...] = jnp.dot(ca, w_ref[...].astype(BF16), preferred_element_type=F32) + b_ref[...]


def _adaln(c, ada_w, ada_b):
    b, d = c.shape
    n = ada_w.shape[1]
    return pl.pallas_call(
        _adaln_kernel,
        grid=(n // ADALN_TN,),
        in_specs=[pl.BlockSpec((b, d), lambda j: (0, 0)),
                  pl.BlockSpec((d, ADALN_TN), lambda j: (0, j)),
                  pl.BlockSpec((1, ADALN_TN), lambda j: (0, j))],
        out_specs=pl.BlockSpec((b, ADALN_TN), lambda j: (0, j)),
        out_shape=jax.ShapeDtypeStruct((b, n), F32),
        compiler_params=_cparams(("arbitrary",)),
        name="adaln",
    )(c, ada_w, ada_b.reshape(1, n))


def _modulated_norm(x, nw, shift, scale):
    y = x * lax.rsqrt(jnp.mean(x * x, axis=-1, keepdims=True) + EPS) * nw
    return y * (1.0 + scale) + shift


def _inproj_kernel(x_ref, mod_ref, nw_ref, w_ref, wdt_ref, o_ref, dt_ref, h_scr):
    @pl.when(pl.program_id(1) == 0)
    def _():
        h = _modulated_norm(x_ref[...], nw_ref[...], mod_ref[0, 0:1, :], mod_ref[0, 1:2, :])
        h_scr[...] = h.astype(BF16)
        dt_ref[...] = jnp.dot(h_scr[...], wdt_ref[...], preferred_element_type=F32)

    o_ref[...] = jnp.dot(h_scr[...], w_ref[...].astype(BF16), preferred_element_type=F32).astype(BF16)


def _in_proj(x2, mod3, norm_w, w_main, w_dt, seq):
    t, d = x2.shape
    tm, tn = INPROJ_TM, INPROJ_TN
    return pl.pallas_call(
        _inproj_kernel,
        grid=(t // tm, N_MAIN // tn),
        in_specs=[pl.BlockSpec((tm, d), lambda i, j: (i, 0)),
                  pl.BlockSpec((1, 6, d), lambda i, j: (i * tm // seq, 0, 0)),
                  pl.BlockSpec((1, d), lambda i, j: (0, 0)),
                  pl.BlockSpec((d, tn), lambda i, j: (0, j)),
                  pl.BlockSpec((d, LANES), lambda i, j: (0, 0))],
        out_specs=[pl.BlockSpec((tm, tn), lambda i, j: (i, j)),
                   pl.BlockSpec((tm, LANES), lambda i, j: (i, 0))],
        out_shape=[jax.ShapeDtypeStruct((t, N_MAIN), BF16),
                   jax.ShapeDtypeStruct((t, LANES), F32)],
        scratch_shapes=[pltpu.VMEM((tm, d), BF16)],
        compiler_params=_cparams(("parallel", "arbitrary")),
        name="in_proj",
    )(x2, mod3, norm_w, w_main, w_dt)


def _attn_kernel(q_ref, k_ref, v_ref, pos_ref, freq_ref, qw_ref, kw_ref, o_ref,
                 cos_s, sa_s, qs, ks, vs, acc_s, m_s, l_s, tmp):
    seq = q_ref.shape[0]
    blk = ATTN_BLOCK
    half = ROT_DIM // 2

    @pl.when(pl.program_id(1) == 0)
    def _():
        ang = pos_ref[0].astype(F32) * freq_ref[...]
        lane = lax.broadcasted_iota(jnp.int32, (seq, LANES), 1)
        sn = jnp.sin(ang)
        cos_s[...] = jnp.cos(ang)
        sa_s[...] = jnp.where(lane < half, -sn, jnp.where(lane < ROT_DIM, sn, 0.0))

    sj = lax.broadcasted_iota(jnp.int32, (LANES, LANES), 0)
    si = lax.broadcasted_iota(jnp.int32, (LANES, LANES), 1)
    swap = (((si < half) & (sj == si + half)) | ((si >= half) & (si < ROT_DIM) & (sj == si - half))).astype(BF16)

    def norm_rope(t_ref, w_ref):
        t = t_ref[...].astype(F32)
        y = t * lax.rsqrt(jnp.mean(t * t, axis=-1, keepdims=True) + EPS) * w_ref[...]
        y_hi = y.astype(BF16)
        y_lo = (y - y_hi.astype(F32)).astype(BF16)
        partner = (jnp.dot(y_hi, swap, preferred_element_type=F32)
                   + jnp.dot(y_lo, swap, preferred_element_type=F32))
        return y * cos_s[...] + partner * sa_s[...]

    qseg = seq // 4
    slab = blk // 4

    def to_quarter(val, dst):
        tmp[...] = val
        for r in range(4):
            dst[r * qseg:(r + 1) * qseg, :] = tmp[pl.ds(r, qseg, stride=4), :]

    to_quarter(norm_rope(q_ref, qw_ref), qs)
    to_quarter(norm_rope(k_ref, kw_ref), ks)
    to_quarter(v_ref[...].astype(F32), vs)

    qi = lax.broadcasted_iota(jnp.int32, (blk, blk), 0)
    kj = lax.broadcasted_iota(jnp.int32, (blk, blk), 1)
    pq = 4 * (qi % slab) + qi // slab
    pk = 4 * (kj % slab) + kj // slab
    masks = {
        "seq": (kj <= qi, jnp.concatenate([kj >= qi, kj <= qi], axis=1)),
        "slab": (pk <= pq, jnp.concatenate([pk >= pq, pk <= pq], axis=1)),
    }
    ones_blk = jnp.ones((blk, LANES), BF16)
    scale = HEAD_DIM ** -0.5
    nt = (((1,), (1,)), ((), ()))

    def load(ref, kind, start):
        if kind == "c":
            return ref[pl.ds(start, blk), :]
        if kind == "s":
            return ref[pl.ds(start, blk, stride=4), :]
        return jnp.concatenate([ref[pl.ds(start + r * qseg, slab), :] for r in range(4)], axis=0)

    def store(ref, kind, start, val):
        if kind == "c":
            ref[pl.ds(start, blk), :] = val
        elif kind == "s":
            ref[pl.ds(start, blk, stride=4), :] = val
        else:
            for r in range(4):
                ref[pl.ds(start + r * qseg, slab), :] = val[r * slab:(r + 1) * slab]

    def group(kind, starts, with_prev, first_pattern):
        step = slab if kind == "q" else blk
        qb = [load(qs, kind, s).astype(BF16) for s in starts]
        kb = [load(ks, kind, s).astype(BF16) for s in starts]
        vb = [jnp.concatenate([load(vs, kind, s).astype(BF16), ones_blk], axis=1) for s in starts]
        if with_prev:
            kb = [jnp.concatenate([load(ks, kind, s - step).astype(BF16), k], axis=0)
                  for s, k in zip(starts, kb)]
            vb = [jnp.concatenate([jnp.concatenate([load(vs, kind, s - step).astype(BF16), ones_blk], axis=1),
                                   v], axis=0) for s, v in zip(starts, vb)]
        mask = masks["slab" if kind == "q" else "seq"][1 if with_prev else 0]
        sc = [jnp.where(mask, lax.dot_general(q, k, nt, preferred_element_type=F32) * scale, -jnp.inf)
              for q, k in zip(qb, kb)]
        if with_prev:
            mx = [jnp.max(jnp.maximum(x[:, :blk], x[:, blk:]), axis=-1, keepdims=True) for x in sc]
        else:
            mx = [jnp.max(x, axis=-1, keepdims=True) for x in sc]
        pr = [jnp.exp(x - m).astype(BF16) for x, m in zip(sc, mx)]
        ov = [jnp.dot(p, v, preferred_element_type=F32) for p, v in zip(pr, vb)]
        mb = [jnp.broadcast_to(m, (blk, LANES)) for m in mx]
        if first_pattern:
            new = [(m, o[:, LANES:], o[:, :LANES]) for m, o in zip(mb, ov)]
        else:
            old = [(load(m_s, kind, s), load(l_s, kind, s), load(acc_s, kind, s)) for s in starts]
            new = []
            for (m_old, l_old, acc_old), m, o in zip(old, mb, ov):
                m_new = jnp.maximum(m_old, m)
                a_old = jnp.exp(m_old - m_new)
                a_new = jnp.exp(m - m_new)
                new.append((m_new, a_old * l_old + a_new * o[:, LANES:],
                            a_old * acc_old + a_new * o[:, :LANES]))
        for s, (m_new, l_new, acc_new) in zip(starts, new):
            store(m_s, kind, s, m_new)
            store(l_s, kind, s, l_new)
            store(acc_s, kind, s, acc_new)

    def loop(n_iter, kind, starts_of, with_prev, first_pattern):
        def body(i, carry):
            group(kind, starts_of(i), with_prev, first_pattern)
            return carry
        lax.fori_loop(0, n_iter, body, 0)

    nb4 = qseg // blk
    group("c", [r * qseg for r in range(4)], False, True)
    loop(2, "c", lambda i: [(2 * i + j) * qseg + n * blk for j in range(2) for n in range(1, nb4)], True, True)
    loop(4, "s", lambda r: [r * qseg + a for a in range(4)], False, False)
    group("q", [0], False, False)
    loop((seq // blk - 1) // ATTN_GROUP, "q",
         lambda i: [(1 + ATTN_GROUP * i + j) * slab for j in range(ATTN_GROUP)], True, False)

    tmp[...] = acc_s[...] / l_s[...]
    for r in range(4):
        qs[pl.ds(r, qseg, stride=4), :] = tmp[r * qseg:(r + 1) * qseg, :]
    o_ref[...] = qs[...].astype(BF16)


def _attention(proj, pos3, freq, qw, kw, batch, seq):
    t = proj.shape[0]
    hd = HEAD_DIM
    nh = N_ATTN_HEADS
    assert seq % (ATTN_BLOCK * max(DILATIONS)) == 0 and (seq // ATTN_BLOCK - 1) % ATTN_GROUP == 0
    scr = pltpu.VMEM((seq, hd), F32)
    return pl.pallas_call(
        _attn_kernel,
        grid=(batch, nh),
        in_specs=[pl.BlockSpec((seq, hd), lambda b, h: (b, h)),
                  pl.BlockSpec((seq, hd), lambda b, h: (b, nh + h)),
                  pl.BlockSpec((seq, hd), lambda b, h: (b, 2 * nh + h)),
                  pl.BlockSpec((1, seq, 1), lambda b, h: (b, 0, 0)),
                  pl.BlockSpec((1, hd), lambda b, h: (0, 0)),
                  pl.BlockSpec((1, hd), lambda b, h: (0, 0)),
                  pl.BlockSpec((1, hd), lambda b, h: (0, 0))],
        out_specs=pl.BlockSpec((seq, hd), lambda b, h: (b, h)),
        out_shape=jax.ShapeDtypeStruct((t, D_ATTN), BF16),
        scratch_shapes=[scr] * 9,
        compiler_params=_cparams(("parallel", "arbitrary")),
        name="attention",
    )(proj, proj, proj, pos3, freq, qw, kw)


def _split3(v):
    hi = v.astype(BF16)
    r1 = v - hi.astype(F32)
    mid = r1.astype(BF16)
    lo = (r1 - mid.astype(F32)).astype(BF16)
    return hi, mid, lo


def _ssd_kernel(xs_ref, bm_ref, cm_ref, z_ref, dtr_ref, cwx_ref, cwb_ref, cwc_ref,
                cbx_ref, cbb_ref, cbc_ref, dtb_ref, alog_ref, dsk_ref, nw_ref, o_ref,
                xpad, x_s, b_s, c_s, dt_s, da_s, dtb_s, daf_s):
    seq = xs_ref.shape[0]
    ck = SSD_CHUNK
    g = pl.program_id(1)
    pad = SUBLANES

    def conv_silu(src_ref, w_ref, b_ref, dst):
        xpad[0:pad, :] = jnp.zeros((pad, LANES), F32)
        xpad[pad:, :] = src_ref[...].astype(F32)
        y = b_ref[...] + jnp.zeros((seq, LANES), F32)
        for j in range(CONV_WIDTH):
            y = y + w_ref[j:j + 1, :] * xpad[pl.ds(pad - (CONV_WIDTH - 1) + j, seq), :]
        dst[...] = y * jax.nn.sigmoid(y)

    conv_silu(xs_ref, cwx_ref, cbx_ref, x_s)
    conv_silu(bm_ref, cwb_ref, cbb_ref, b_s)
    conv_silu(cm_ref, cwc_ref, cbc_ref, c_s)

    @pl.when(g == 0)
    def _():
        xr = dtr_ref[...] + dtb_ref[...]
        dt = jnp.maximum(xr, 0.0) + jnp.log1p(jnp.exp(-jnp.abs(xr)))
        for i, piece in enumerate(_split3(dt)):
            dt_s[i] = piece
        for i, piece in enumerate(_split3(dt * (-jnp.exp(alog_ref[...])))):
            da_s[i] = piece

    def select(v_s, i, sel):
        return jnp.dot(v_s[i], sel, preferred_element_type=F32)

    hrow = lax.broadcasted_iota(jnp.int32, (LANES, 2 * LANES), 0)
    hlane = lax.broadcasted_iota(jnp.int32, (LANES, 2 * LANES), 1)
    brow = lax.broadcasted_iota(jnp.int32, (LANES, LANES), 0)
    blane = lax.broadcasted_iota(jnp.int32, (LANES, LANES), 1)
    sel_b = (brow == 2 * g + (blane >= SSM_HEAD_DIM).astype(jnp.int32)).astype(BF16)
    sel_f = (hrow == 2 * g + (hlane >= LANES).astype(jnp.int32)).astype(BF16)
    dtb_s[...] = select(dt_s, 0, sel_b) + select(dt_s, 1, sel_b) + select(dt_s, 2, sel_b)
    for i in range(3):
        daf_s[i] = select(da_s, i, sel_f).astype(BF16)

    li = lax.broadcasted_iota(jnp.int32, (ck, ck), 0)
    si = lax.broadcasted_iota(jnp.int32, (ck, ck), 1)
    causal = si <= li
    eye = si == li
    tril_bf = causal.astype(BF16)
    head1 = lax.broadcasted_iota(jnp.int32, (ck, LANES), 1) >= SSM_HEAD_DIM
    head1_row = lax.broadcasted_iota(jnp.int32, (1, LANES), 1) >= SSM_HEAD_DIM

    def chunk(c, prev):
        r = pl.ds(pl.multiple_of(c * ck, ck), ck)
        xc = x_s[r, :]
        bc = b_s[r, :].astype(BF16)
        cc = c_s[r, :].astype(BF16)
        acs2 = (jnp.dot(tril_bf, daf_s[0, r, :], preferred_element_type=F32)
                + jnp.dot(tril_bf, daf_s[1, r, :], preferred_element_type=F32)
                + jnp.dot(tril_bf, daf_s[2, r, :], preferred_element_type=F32))
        acs_f = (acs2[:, :LANES], acs2[:, LANES:])
        acs_b = jnp.where(head1, acs_f[1], acs_f[0])
        last_b = jnp.where(head1_row, acs_f[1][ck - 1:ck, :], acs_f[0][ck - 1:ck, :])
        cb = lax.dot_general(cc, bc, (((1,), (1,)), ((), ())), preferred_element_type=F32)
        xdt = xc * dtb_s[r, :]
        scores = []
        for k in range(2):
            acs_row = jnp.sum(jnp.where(eye, acs_f[k], 0.0), axis=0, keepdims=True)
            decay = jnp.exp(jnp.where(causal, acs_f[k] - acs_row, -jnp.inf))
            scores.append((cb * decay).astype(BF16))
        xdt2 = jnp.concatenate([jnp.where(head1, 0.0, xdt).astype(BF16),
                                jnp.where(head1, xdt, 0.0).astype(BF16)], axis=0)
        y_diag = jnp.dot(jnp.concatenate(scores, axis=1), xdt2, preferred_element_type=F32)
        y_off = jnp.dot(cc, prev.astype(BF16), preferred_element_type=F32) * jnp.exp(acs_b)
        y = y_diag + y_off + dsk_ref[...] * xc
        zc = z_ref[r, :].astype(F32)
        y = y * (zc * jax.nn.sigmoid(zc))
        y = y * lax.rsqrt(jnp.mean(y * y, axis=-1, keepdims=True) + EPS) * nw_ref[...]
        o_ref[r, :] = y.astype(BF16)
        w = (xdt * jnp.exp(last_b - acs_b)).astype(BF16)
        st = lax.dot_general(bc, w, (((0,), (0,)), ((), ())), preferred_element_type=F32)
        return prev * jnp.exp(last_b) + st

    lax.fori_loop(0, seq // ck, chunk, jnp.zeros((SSM_STATE, LANES), F32), unroll=8)


def _ssd(proj, dt_raw, conv_w, conv_b, dtb, alog, dsk, nw, batch, seq):
    t = proj.shape[0]
    gs = SSM_GROUPS
    col = lambda off: (lambda b, g: (b, off + g))
    wcol = lambda off: (lambda b, g: (0, off + g))
    zq, xq, bq, cq = 3 * D_ATTN // LANES, 4 * D_ATTN // LANES, 5 * D_ATTN // LANES, 6 * D_ATTN // LANES
    const = lambda b, g: (0, 0)
    scr = pltpu.VMEM((seq, LANES), F32)
    return pl.pallas_call(
        _ssd_kernel,
        grid=(batch, gs),
        in_specs=[pl.BlockSpec((seq, LANES), col(xq)),
                  pl.BlockSpec((seq, LANES), col(bq)),
                  pl.BlockSpec((seq, LANES), col(cq)),
                  pl.BlockSpec((seq, LANES), col(zq)),
                  pl.BlockSpec((seq, LANES), lambda b, g: (b, 0)),
                  pl.BlockSpec((CONV_WIDTH, LANES), wcol(0)),
                  pl.BlockSpec((CONV_WIDTH, LANES), wcol(gs)),
                  pl.BlockSpec((CONV_WIDTH, LANES), wcol(2 * gs)),
                  pl.BlockSpec((1, LANES), wcol(0)),
                  pl.BlockSpec((1, LANES), wcol(gs)),
                  pl.BlockSpec((1, LANES), wcol(2 * gs)),
                  pl.BlockSpec((1, LANES), const),
                  pl.BlockSpec((1, LANES), const),
                  pl.BlockSpec((1, LANES), lambda b, g: (0, g)),
                  pl.BlockSpec((1, LANES), lambda b, g: (0, g))],
        out_specs=pl.BlockSpec((seq, LANES), lambda b, g: (b, g)),
        out_shape=jax.ShapeDtypeStruct((t, D_SSM), BF16),
        scratch_shapes=[pltpu.VMEM((seq + SUBLANES, LANES), F32), scr, scr, scr,
                        pltpu.VMEM((3, seq, LANES), BF16), pltpu.VMEM((3, seq, LANES), BF16), scr,
                        pltpu.VMEM((3, seq, 2 * LANES), BF16)],
        compiler_params=_cparams(("arbitrary", "arbitrary")),
        name="ssd",
    )(proj, proj, proj, proj, dt_raw, conv_w, conv_w, conv_w, conv_b, conv_b, conv_b,
      dtb, alog, dsk, nw)


def _out_router_kernel(ya_ref, ys_ref, wa_ref, ws_ref, x_ref, mod_ref, nw_ref, rw_ref, rb_ref,
                       x1_ref, h2_ref, eid_ref, ew_ref, rank_ref, cnt_ref, cnt_s):
    tm = x_ref.shape[0]
    parts = []
    for r0 in range(0, tm, OUT_SUB):
        rows = slice(r0, r0 + OUT_SUB)
        mix = (jnp.dot(ya_ref[rows, :], wa_ref[...], preferred_element_type=F32)
               + jnp.dot(ys_ref[rows, :], ws_ref[...], preferred_element_type=F32))
        x1 = x_ref[rows, :] + mod_ref[0, 2:3, :] * mix
        x1_ref[rows, :] = x1
        part = _modulated_norm(x1, nw_ref[...], mod_ref[0, 3:4, :], mod_ref[0, 4:5, :])
        h2_ref[rows, :] = _pack_bf16_pair(part)
        parts.append(part)
    h2 = jnp.concatenate(parts, axis=0)

    h_hi = h2.astype(BF16)
    h_lo = (h2 - h_hi.astype(F32)).astype(BF16)
    rw = rw_ref[...]
    w_hi = rw.astype(BF16)
    w_lo = (rw - w_hi.astype(F32)).astype(BF16)
    nt = (((1,), (1,)), ((), ()))
    logits = (lax.dot_general(w_hi, h_hi, nt, preferred_element_type=F32)
              + lax.dot_general(w_hi, h_lo, nt, preferred_element_type=F32)
              + lax.dot_general(w_lo, h_hi, nt, preferred_element_type=F32)) + rb_ref[...]

    ng, ne = N_EXPERT_GROUPS, EXPERTS_PER_GROUP
    iota = lax.broadcasted_iota(jnp.int32, (ne, tm), 0)

    def argmax0(v):
        mx = jnp.max(v, axis=0, keepdims=True)
        idx = jnp.min(jnp.where(v == mx, iota, ne), axis=0, keepdims=True)
        return mx, idx

    gl = logits[N_EXPERTS:N_EXPERTS + ng, :]
    gmax, gidx = argmax0(gl)
    g_gate = 1.0 / jnp.sum(jnp.exp(gl - gmax), axis=0, keepdims=True)
    e_in = jnp.zeros((ne, tm), F32)
    for gi in range(ng):
        e_in = e_in + jnp.where(gidx == gi, logits[gi * ne:(gi + 1) * ne, :], 0.0)
    m1, i1 = argmax0(e_in)
    m2, i2 = argmax0(jnp.where(iota == i1, -jnp.inf, e_in))
    p2 = jnp.exp(m2 - m1)
    den = 1.0 + p2
    e0 = gidx * ne + i1
    e1 = gidx * ne + i2
    eid_ref[0, 0:1, :] = e0
    eid_ref[0, 1:2, :] = e1
    ew_ref[0:1, :] = (1.0 / den) * g_gate
    ew_ref[1:2, :] = (p2 / den) * g_gate

    @pl.when(pl.program_id(0) == 0)
    def _():
        cnt_s[...] = jnp.zeros(cnt_s.shape, F32)

    eiota = lax.broadcasted_iota(jnp.int32, (N_EXPERTS, tm), 0)
    oh0 = eiota == e0
    oh1 = eiota == e1
    both = jnp.logical_or(oh0, oh1)
    ti = lax.broadcasted_iota(jnp.int32, (tm, tm), 0)
    tj = lax.broadcasted_iota(jnp.int32, (tm, tm), 1)
    earlier = (ti < tj).astype(BF16)
    before = cnt_s[...] + jnp.dot(both.astype(BF16), earlier, preferred_element_type=F32)
    rank_ref[0, 0:1, :] = jnp.sum(jnp.where(oh0, before, 0.0), axis=0, keepdims=True).astype(jnp.int32)
    rank_ref[0, 1:2, :] = jnp.sum(jnp.where(oh1, before, 0.0), axis=0, keepdims=True).astype(jnp.int32)
    total = cnt_s[...] + jnp.sum(both.astype(F32), axis=1, keepdims=True)
    cnt_s[...] = total
    cnt_ref[...] = total.astype(jnp.int32)


def _out_router(ya, ys, wa, ws, x2, mod3, norm_w, rw, rb, seq):
    t, d = x2.shape
    tm = OUT_TM
    const = lambda i: (0, 0)
    return pl.pallas_call(
        _out_router_kernel,
        grid=(t // tm,),
        in_specs=[pl.BlockSpec((tm, D_ATTN), lambda i: (i, 0)),
                  pl.BlockSpec((tm, D_SSM), lambda i: (i, 0)),
                  pl.BlockSpec((D_ATTN, d), const, pipeline_mode=pl.Buffered(1)),
                  pl.BlockSpec((D_SSM, d), const, pipeline_mode=pl.Buffered(1)),
                  pl.BlockSpec((tm, d), lambda i: (i, 0)),
                  pl.BlockSpec((1, 6, d), lambda i: (i * tm // seq, 0, 0)),
                  pl.BlockSpec((1, d), const),
                  pl.BlockSpec((LANES, d), const, pipeline_mode=pl.Buffered(1)),
                  pl.BlockSpec((LANES, 1), const)],
        out_specs=[pl.BlockSpec((tm, d), lambda i: (i, 0)),
                   pl.BlockSpec((tm, d // 2), lambda i: (i, 0)),
                   pl.BlockSpec((1, 2, tm), lambda i: (i, 0, 0)),
                   pl.BlockSpec((2, tm), lambda i: (0, i)),
                   pl.BlockSpec((1, 2, tm), lambda i: (i, 0, 0)),
                   pl.BlockSpec((N_EXPERTS, 1), const)],
        out_shape=[jax.ShapeDtypeStruct((t, d), F32),
                   jax.ShapeDtypeStruct((t, d // 2), jnp.uint32),
                   jax.ShapeDtypeStruct((t // tm, 2, tm), jnp.int32),
                   jax.ShapeDtypeStruct((2, t), F32),
                   jax.ShapeDtypeStruct((t // tm, 2, tm), jnp.int32),
                   jax.ShapeDtypeStruct((N_EXPERTS, 1), jnp.int32)],
        scratch_shapes=[pltpu.VMEM((N_EXPERTS, 1), F32)],
        compiler_params=_cparams(("arbitrary",)),
        name="out_router",
    )(ya, ys, wa, ws, x2, mod3, norm_w, rw, rb)


def _for_each_assignment(tm, body):
    def group(gi, carry):
        for jj in range(SUBLANES):
            for k in range(2):
                body(k, gi, jj, k * tm + gi * SUBLANES + jj)
        return carry
    lax.fori_loop(0, tm // SUBLANES, group, 0)


def _rows3(a):
    return a.reshape(a.shape[0] // SUBLANES, SUBLANES, a.shape[1])


def _scatter_kernel(zstart_ref, dest_ref, h_ref, xd_hbm, sem, zbuf, zsem):
    tm = h_ref.shape[0] * SUBLANES

    @pl.when(pl.program_id(0) == 0)
    def _():
        zbuf[...] = jnp.zeros(zbuf.shape, zbuf.dtype)

        def zero_copy(j):
            start = pl.multiple_of(jnp.maximum(zstart_ref[j], 0), MOE_ROWS)
            return pltpu.make_async_copy(zbuf, xd_hbm.at[pl.ds(start, MOE_ROWS)], zsem)

        def z_issue(j, carry):
            @pl.when(zstart_ref[j] >= 0)
            def _():
                zero_copy(j).start()
            return carry

        def z_wait(j, carry):
            @pl.when(zstart_ref[j] >= 0)
            def _():
                zero_copy(j).wait()
            return carry

        lax.fori_loop(0, zstart_ref.shape[0], z_issue, 0)
        lax.fori_loop(0, zstart_ref.shape[0], z_wait, 0)

    def issue(k, gi, jj, j):
        pltpu.make_async_copy(h_ref.at[gi, pl.ds(jj, 1)], xd_hbm.at[pl.ds(dest_ref[0, 0, j], 1)], sem).start()

    _for_each_assignment(tm, issue)
    for _ in range(2 * SUBLANES):
        pltpu.make_async_copy(h_ref.at[:, 0], xd_hbm.at[pl.ds(0, tm // SUBLANES)], sem).wait()


def _scatter_rows(zstart, dest3, h2, n_rows):
    t, d = h2.shape
    n_tiles = dest3.shape[0]
    tm = dest3.shape[2] // 2
    grid_spec = pltpu.PrefetchScalarGridSpec(
        num_scalar_prefetch=1,
        grid=(n_tiles,),
        in_specs=[pl.BlockSpec((1, 1, 2 * tm), lambda i, zs: (i, 0, 0), memory_space=pltpu.SMEM),
                  pl.BlockSpec((tm // SUBLANES, SUBLANES, d), lambda i, zs: (i, 0, 0))],
        out_specs=pl.BlockSpec(memory_space=pl.ANY),
        scratch_shapes=[pltpu.SemaphoreType.DMA(()), pltpu.VMEM((MOE_ROWS, d), h2.dtype),
                        pltpu.SemaphoreType.DMA(())],
    )
    return pl.pallas_call(
        _scatter_kernel,
        grid_spec=grid_spec,
        out_shape=jax.ShapeDtypeStruct((n_rows, d), h2.dtype),
        compiler_params=_cparams(("arbitrary",)),
        name="scatter_rows",
    )(zstart, dest3, _rows3(h2))


def _experts_kernel(be_ref, nv_ref, first_ref, nxt_ref, slot_ref, x_ref, wg_hbm, wu_hbm, wd_hbm, o_ref,
                    wg_f, wu_f, wd_f, sems, wg_s, wu_s, wd_s):
    i = pl.program_id(0)
    n_valid = nv_ref[i]
    half = x_ref.shape[1]

    def weight_copies(e, slot):
        return (pltpu.make_async_copy(wg_hbm.at[e], wg_f.at[slot], sems.at[slot, 0]),
                pltpu.make_async_copy(wu_hbm.at[e], wu_f.at[slot], sems.at[slot, 1]),
                pltpu.make_async_copy(wd_hbm.at[e], wd_f.at[slot], sems.at[slot, 2]))

    @pl.when(i == 0)
    def _():
        for c in weight_copies(be_ref[0], slot_ref[0]):
            c.start()

    @pl.when(first_ref[i] == 1)
    def _():
        slot = slot_ref[i]
        for c in weight_copies(be_ref[i], slot):
            c.wait()

        @pl.when(nxt_ref[i] >= 0)
        def _():
            for c in weight_copies(nxt_ref[i], 1 - slot):
                c.start()

        wg_s[...] = wg_f[slot].astype(BF16)
        wu_s[...] = wu_f[slot].astype(BF16)
        wd_s[...] = wd_f[slot].astype(BF16)

    @pl.when(n_valid > 0)
    def _():
        x_lo, x_hi = _unpack_bf16_pair(x_ref[...])
        x_lo = x_lo.astype(BF16)
        x_hi = x_hi.astype(BF16)
        gt = (jnp.dot(x_lo, wg_s[0:half, :], preferred_element_type=F32)
              + jnp.dot(x_hi, wg_s[half:, :], preferred_element_type=F32))
        up = (jnp.dot(x_lo, wu_s[0:half, :], preferred_element_type=F32)
              + jnp.dot(x_hi, wu_s[half:, :], preferred_element_type=F32))
        hid = (gt * jax.nn.sigmoid(gt)) * up
        o_ref[...] = _pack_bf16_pair(jnp.dot(hid.astype(BF16), wd_s[...], preferred_element_type=F32))

    @pl.when(n_valid == 0)
    def _():
        o_ref[...] = jnp.zeros(o_ref.shape, o_ref.dtype)


def _experts(block_e, n_valid, first, nxt, slot, xd, w_gate, w_up, w_down):
    rows = MOE_ROWS
    n_blocks = xd.shape[0] // rows
    half = xd.shape[1]
    d = 2 * half
    de = w_gate.shape[2]
    blk = lambda i, *_: (i, 0)
    grid_spec = pltpu.PrefetchScalarGridSpec(
        num_scalar_prefetch=5,
        grid=(n_blocks,),
        in_specs=[pl.BlockSpec((rows, half), blk),
                  pl.BlockSpec(memory_space=pl.ANY),
                  pl.BlockSpec(memory_space=pl.ANY),
                  pl.BlockSpec(memory_space=pl.ANY)],
        out_specs=pl.BlockSpec((rows, half), blk),
        scratch_shapes=[pltpu.VMEM((2, d, de), F32),
                        pltpu.VMEM((2, d, de), F32),
                        pltpu.VMEM((2, de, d), F32),
                        pltpu.SemaphoreType.DMA((2, 3)),
                        pltpu.VMEM((d, de), BF16),
                        pltpu.VMEM((d, de), BF16),
                        pltpu.VMEM((de, d), BF16)],
    )
    return pl.pallas_call(
        _experts_kernel,
        grid_spec=grid_spec,
        out_shape=jax.ShapeDtypeStruct((n_blocks * rows, half), jnp.uint32),
        compiler_params=_cparams(("arbitrary",)),
        name="experts",
    )(block_e, n_valid, first, nxt, slot, xd, w_gate, w_up, w_down)


def _combine_kernel(dest_ref, destn_ref, yd_hbm, x1_ref, mod_ref, ew_ref, o_ref, ybuf, sems):
    i = pl.program_id(0)
    n = pl.num_programs(0)
    tm = x1_ref.shape[0]
    tg = tm // SUBLANES
    slot = i % 2

    def gather(d_ref, s):
        def issue(k, gi, jj, j):
            pltpu.make_async_copy(yd_hbm.at[pl.ds(d_ref[0, 0, j], 1)],
                                  ybuf.at[s, k * tg + gi, pl.ds(jj, 1)], sems.at[s]).start()
        _for_each_assignment(tm, issue)

    @pl.when(i == 0)
    def _():
        gather(dest_ref, 0)

    @pl.when(i + 1 < n)
    def _():
        gather(destn_ref, 1 - slot)

    for _ in range(SUBLANES):
        pltpu.make_async_copy(yd_hbm.at[pl.ds(0, 2 * tg)], ybuf.at[slot, :, 0], sems.at[slot]).wait()
    half = ybuf.shape[3]
    y0_lo, y0_hi = _unpack_bf16_pair(ybuf[slot, 0:tg].reshape(tm, half))
    y1_lo, y1_hi = _unpack_bf16_pair(ybuf[slot, tg:2 * tg].reshape(tm, half))
    w0 = ew_ref[:, 0:1]
    w1 = ew_ref[:, 1:2]
    o_ref[:, 0:half] = x1_ref[:, 0:half] + mod_ref[0, 5:6, 0:half] * (w0 * y0_lo + w1 * y1_lo)
    o_ref[:, half:] = x1_ref[:, half:] + mod_ref[0, 5:6, half:] * (w0 * y0_hi + w1 * y1_hi)


def _combine(dest3, yd, x1, mod3, ew_t, seq):
    t, d = x1.shape
    n = dest3.shape[0]
    tm = dest3.shape[2] // 2
    smem = lambda im: pl.BlockSpec((1, 1, 2 * tm), im, memory_space=pltpu.SMEM)
    return pl.pallas_call(
        _combine_kernel,
        grid=(n,),
        in_specs=[smem(lambda i: (i, 0, 0)), smem(lambda i: (jnp.minimum(i + 1, n - 1), 0, 0)),
                  pl.BlockSpec(memory_space=pl.ANY),
                  pl.BlockSpec((tm, d), lambda i: (i, 0)),
                  pl.BlockSpec((1, 6, d), lambda i: (i * tm // seq, 0, 0)),
                  pl.BlockSpec((tm, 2), lambda i: (i, 0))],
        out_specs=pl.BlockSpec((tm, d), lambda i: (i, 0)),
        out_shape=jax.ShapeDtypeStruct((t, d), F32),
        scratch_shapes=[pltpu.VMEM((2, 2 * tm // SUBLANES, SUBLANES, yd.shape[1]), yd.dtype),
                        pltpu.SemaphoreType.DMA((2,))],
        compiler_params=_cparams(("arbitrary",)),
        name="combine",
    )(dest3, dest3, yd, x1, mod3, ew_t)


def _layout(counts, n_assign):
    rows = MOE_ROWS
    n_blocks = (n_assign + N_EXPERTS * (rows - 1)) // rows
    padded = (counts + rows - 1) // rows * rows
    pend = jnp.cumsum(padded)
    pstart = pend - padded
    bstart = jnp.arange(n_blocks, dtype=jnp.int32) * rows
    block_e = jnp.minimum(jnp.sum((pend[None, :] <= bstart[:, None]).astype(jnp.int32), axis=1),
                          N_EXPERTS - 1)
    n_valid = jnp.clip(pstart[block_e] + counts[block_e] - bstart, 0, rows)
    n_valid = jnp.where(bstart < pend[-1], n_valid, 0)
    min_used = n_assign // rows
    tail = pend[-1] + jnp.arange(n_blocks - min_used, dtype=jnp.int32) * rows
    zstart = jnp.concatenate([jnp.where(counts > 0, pend - rows, -1),
                              jnp.where(tail < n_blocks * rows, tail, -1)])
    first = (n_valid > 0) & jnp.concatenate([jnp.ones((1,), bool), block_e[1:] != block_e[:-1]])
    slot = (jnp.cumsum(first.astype(jnp.int32)) - 1) % 2
    ids = jnp.where(counts > 0, jnp.arange(N_EXPERTS, dtype=jnp.int32), N_EXPERTS)
    later = jnp.concatenate([lax.cummin(ids[::-1])[::-1][1:], jnp.full((1,), N_EXPERTS, jnp.int32)])
    nxt = jnp.where(later < N_EXPERTS, later, -1)[block_e]
    sched = (first.astype(jnp.int32), nxt.astype(jnp.int32), jnp.maximum(slot, 0).astype(jnp.int32))
    return (pstart.astype(jnp.int32), zstart.astype(jnp.int32), block_e.astype(jnp.int32),
            n_valid.astype(jnp.int32), sched, n_blocks * rows)


def kernel(x, c, positions, ada_w, ada_b, norm1_w, w_in, q_norm_w, k_norm_w, conv_w, conv_b,
           dt_bias, a_log, d_skip, ssm_norm_w, w_out, norm2_w, router_group_w, router_group_b,
           router_expert_w, router_expert_b, w_gate, w_up, w_down):
    batch, seq, d = x.shape
    t = batch * seq
    depth = ada_w.shape[0]
    nh_pad = LANES - N_SSM_HEADS
    inv_freq = ROPE_THETA ** (-(jnp.arange(0, ROT_DIM, 2, dtype=F32) / ROT_DIM))
    freq = jnp.concatenate([inv_freq, inv_freq, jnp.zeros((HEAD_DIM - ROT_DIM,), F32)]).reshape(1, HEAD_DIM)
    pos3 = positions.reshape(batch, seq, 1)
    xc = x.reshape(t, d)
    for l in range(depth):
        mod3 = _adaln(c, ada_w[l], ada_b[l]).reshape(batch, 6, d)
        w_main = w_in[l]
        w_dt = jnp.pad(w_in[l][:, N_MAIN:], ((0, 0), (0, nh_pad))).astype(BF16)
        proj, dt_raw = _in_proj(xc, mod3, norm1_w[l].reshape(1, d), w_main, w_dt, seq)
        y_attn = _attention(proj, pos3, freq, q_norm_w[l].reshape(1, HEAD_DIM),
                            k_norm_w[l].reshape(1, HEAD_DIM), batch, seq)
        y_ssm = _ssd(proj, dt_raw, conv_w[l], conv_b[l].reshape(1, -1),
                     jnp.pad(dt_bias[l], (0, nh_pad)).reshape(1, LANES),
                     jnp.pad(a_log[l], (0, nh_pad)).reshape(1, LANES),
                     jnp.repeat(d_skip[l], SSM_HEAD_DIM).reshape(1, D_SSM),
                     ssm_norm_w[l].reshape(1, D_SSM), batch, seq)
        rw = jnp.concatenate([router_expert_w[l].T, router_group_w[l].T,
                              jnp.zeros((LANES - N_EXPERTS - N_EXPERT_GROUPS, d), F32)])
        rb = jnp.concatenate([router_expert_b[l], router_group_b[l],
                              jnp.zeros((LANES - N_EXPERTS - N_EXPERT_GROUPS,), F32)]).reshape(LANES, 1)
        wo = w_out[l].astype(BF16)
        x1, h2, eid3, ew, rank3, counts = _out_router(y_attn, y_ssm, wo[:D_ATTN], wo[D_ATTN:], xc, mod3,
                                                      norm2_w[l].reshape(1, d), rw, rb, seq)
        pstart, zstart, block_e, n_valid, sched, n_rows = _layout(counts.reshape(N_EXPERTS), 2 * t)
        seg = jnp.sum(jnp.where(eid3[..., None] == jnp.arange(N_EXPERTS, dtype=jnp.int32), pstart, 0), axis=-1)
        dest3 = (seg + rank3).reshape(t // OUT_TM, 1, 2 * OUT_TM)
        xd = _scatter_rows(zstart, dest3, h2, n_rows)
        yd = _experts(block_e, n_valid, *sched, xd, w_gate[l], w_up[l], w_down[l])
        xc = _combine(dest3, yd, x1, mod3, ew.T, seq)
    return xc.reshape(batch, seq, d)
```

```python
import jax
import jax.numpy as jnp
from jax import lax
from jax.experimental import pallas as pl
from jax.experimental.pallas import tpu as pltpu

F32 = jnp.float32
BF16 = jnp.bfloat16

D_MODEL = 2048
D_ATTN = 1024
D_SSM = 1024
HEAD_DIM = 128
N_ATTN_HEADS = 8
ROT_DIM = 32
ROPE_THETA = 500000.0
ATTN_BLOCK = 128
DILATIONS = (1, 4, 16)
SSM_HEAD_DIM = 64
N_SSM_HEADS = 16
SSM_GROUPS = 8
SSM_STATE = 128
CONV_WIDTH = 4
SSD_CHUNK = 128
N_MAIN = 3 * D_ATTN + D_SSM + D_SSM + 2 * SSM_GROUPS * SSM_STATE
N_EXPERT_GROUPS = 8
EXPERTS_PER_GROUP = 8
N_EXPERTS = 64
D_EXPERT = 512
EPS = 1e-6

LANES = 128
SUBLANES = 8
VMEM_LIMIT = 56 * 1024 * 1024

ADALN_TN = 1024
INPROJ_TM = 1024
INPROJ_TN = 1024
ATTN_GROUP = 15
OUT_TM = 512
OUT_SUB = 128
MOE_ROWS = 256


def _cparams(sem):
    return pltpu.CompilerParams(dimension_semantics=sem, vmem_limit_bytes=VMEM_LIMIT)


def _pack_bf16_pair(x):
    k = x.shape[1] // 2
    lo = lax.bitcast_convert_type(x[:, :k].astype(BF16).astype(F32), jnp.uint32)
    hi = lax.bitcast_convert_type(x[:, k:].astype(BF16).astype(F32), jnp.uint32)
    return (lo >> 16) | (hi & jnp.uint32(0xFFFF0000))


def _unpack_bf16_pair(p):
    lo = lax.bitcast_convert_type(p << 16, F32)
    hi = lax.bitcast_convert_type(p & jnp.uint32(0xFFFF0000), F32)
    return lo, hi


def _adaln_kernel(c_ref, w_ref, b_ref, o_ref):
    c = c_ref[...]
    ca = (c * jax.nn.sigmoid(c)).astype(BF16)
    o_ref[...] = jnp.dot(ca, w_ref[...].astype(BF16), preferred_element_type=F32) + b_ref[...]


def _adaln(c, ada_w, ada_b):
    b, d = c.shape
    n = ada_w.shape[1]
    return pl.pallas_call(
        _adaln_kernel,
        grid=(n // ADALN_TN,),
        in_specs=[pl.BlockSpec((b, d), lambda j: (0, 0)),
                  pl.BlockSpec((d, ADALN_TN), lambda j: (0, j)),
                  pl.BlockSpec((1, ADALN_TN), lambda j: (0, j))],
        out_specs=pl.BlockSpec((b, ADALN_TN), lambda j: (0, j)),
        out_shape=jax.ShapeDtypeStruct((b, n), F32),
        compiler_params=_cparams(("arbitrary",)),
        name="adaln",
    )(c, ada_w, ada_b.reshape(1, n))


def _modulated_norm(x, nw, shift, scale):
    y = x * lax.rsqrt(jnp.mean(x * x, axis=-1, keepdims=True) + EPS) * nw
    return y * (1.0 + scale) + shift


def _inproj_kernel(x_ref, mod_ref, nw_ref, w_ref, wdt_ref, o_ref, dt_ref, h_scr):
    @pl.when(pl.program_id(1) == 0)
    def _():
        h = _modulated_norm(x_ref[...], nw_ref[...], mod_ref[0, 0:1, :], mod_ref[0, 1:2, :])
        h_scr[...] = h.astype(BF16)
        dt_ref[...] = jnp.dot(h_scr[...], wdt_ref[...], preferred_element_type=F32)

    o_ref[...] = jnp.dot(h_scr[...], w_ref[...], preferred_element_type=F32).astype(BF16)


def _in_proj(x2, mod3, norm_w, w_main, w_dt, seq):
    t, d = x2.shape
    tm, tn = INPROJ_TM, INPROJ_TN
    return pl.pallas_call(
        _inproj_kernel,
        grid=(t // tm, N_MAIN // tn),
        in_specs=[pl.BlockSpec((tm, d), lambda i, j: (i, 0)),
                  pl.BlockSpec((1, 6, d), lambda i, j: (i * tm // seq, 0, 0)),
                  pl.BlockSpec((1, d), lambda i, j: (0, 0)),
                  pl.BlockSpec((d, tn), lambda i, j: (0, j)),
                  pl.BlockSpec((d, LANES), lambda i, j: (0, 0))],
        out_specs=[pl.BlockSpec((tm, tn), lambda i, j: (i, j)),
                   pl.BlockSpec((tm, LANES), lambda i, j: (i, 0))],
        out_shape=[jax.ShapeDtypeStruct((t, N_MAIN), BF16),
                   jax.ShapeDtypeStruct((t, LANES), F32)],
        scratch_shapes=[pltpu.VMEM((tm, d), BF16)],
        compiler_params=_cparams(("parallel", "arbitrary")),
        name="in_proj",
    )(x2, mod3, norm_w, w_main, w_dt)


def _attn_kernel(q_ref, k_ref, v_ref, pos_ref, freq_ref, qw_ref, kw_ref, o_ref,
                 cos_s, sa_s, qs, ks, vs, acc_s, m_s, l_s, tmp):
    seq = q_ref.shape[0]
    blk = ATTN_BLOCK
    half = ROT_DIM // 2

    @pl.when(pl.program_id(1) == 0)
    def _():
        ang = pos_ref[0].astype(F32) * freq_ref[...]
        lane = lax.broadcasted_iota(jnp.int32, (seq, LANES), 1)
        sn = jnp.sin(ang)
        cos_s[...] = jnp.cos(ang)
        sa_s[...] = jnp.where(lane < half, -sn, jnp.where(lane < ROT_DIM, sn, 0.0))

    sj = lax.broadcasted_iota(jnp.int32, (LANES, LANES), 0)
    si = lax.broadcasted_iota(jnp.int32, (LANES, LANES), 1)
    swap = (((si < half) & (sj == si + half)) | ((si >= half) & (si < ROT_DIM) & (sj == si - half))).astype(BF16)

    def norm_rope(t_ref, w_ref):
        t = t_ref[...].astype(F32)
        y = t * lax.rsqrt(jnp.mean(t * t, axis=-1, keepdims=True) + EPS) * w_ref[...]
        y_hi = y.astype(BF16)
        y_lo = (y - y_hi.astype(F32)).astype(BF16)
        partner = (jnp.dot(y_hi, swap, preferred_element_type=F32)
                   + jnp.dot(y_lo, swap, preferred_element_type=F32))
        return y * cos_s[...] + partner * sa_s[...]

    qseg = seq // 4
    slab = blk // 4

    def to_quarter(val, dst):
        tmp[...] = val
        for r in range(4):
            dst[r * qseg:(r + 1) * qseg, :] = tmp[pl.ds(r, qseg, stride=4), :]

    to_quarter(norm_rope(q_ref, qw_ref), qs)
    to_quarter(norm_rope(k_ref, kw_ref), ks)
    to_quarter(v_ref[...].astype(F32), vs)

    qi = lax.broadcasted_iota(jnp.int32, (blk, blk), 0)
    kj = lax.broadcasted_iota(jnp.int32, (blk, blk), 1)
    pq = 4 * (qi % slab) + qi // slab
    pk = 4 * (kj % slab) + kj // slab
    masks = {
        "seq": (kj <= qi, jnp.concatenate([kj >= qi, kj <= qi], axis=1)),
        "slab": (pk <= pq, jnp.concatenate([pk >= pq, pk <= pq], axis=1)),
    }
    ones_blk = jnp.ones((blk, LANES), BF16)
    scale = HEAD_DIM ** -0.5
    nt = (((1,), (1,)), ((), ()))

    def load(ref, kind, start):
        if kind == "c":
            return ref[pl.ds(start, blk), :]
        if kind == "s":
            return ref[pl.ds(start, blk, stride=4), :]
        return jnp.concatenate([ref[pl.ds(start + r * qseg, slab), :] for r in range(4)], axis=0)

    def store(ref, kind, start, val):
        if kind == "c":
            ref[pl.ds(start, blk), :] = val
        elif kind == "s":
            ref[pl.ds(start, blk, stride=4), :] = val
        else:
            for r in range(4):
                ref[pl.ds(start + r * qseg, slab), :] = val[r * slab:(r + 1) * slab]

    def group(kind, starts, with_prev, first_pattern):
        step = slab if kind == "q" else blk
        qb = [load(qs, kind, s).astype(BF16) for s in starts]
        kb = [load(ks, kind, s).astype(BF16) for s in starts]
        vb = [jnp.concatenate([load(vs, kind, s).astype(BF16), ones_blk], axis=1) for s in starts]
        if with_prev:
            kb = [jnp.concatenate([load(ks, kind, s - step).astype(BF16), k], axis=0)
                  for s, k in zip(starts, kb)]
            vb = [jnp.concatenate([jnp.concatenate([load(vs, kind, s - step).astype(BF16), ones_blk], axis=1),
                                   v], axis=0) for s, v in zip(starts, vb)]
        mask = masks["slab" if kind == "q" else "seq"][1 if with_prev else 0]
        sc = [jnp.where(mask, lax.dot_general(q, k, nt, preferred_element_type=F32) * scale, -jnp.inf)
              for q, k in zip(qb, kb)]
        if with_prev:
            mx = [jnp.max(jnp.maximum(x[:, :blk], x[:, blk:]), axis=-1, keepdims=True) for x in sc]
        else:
            mx = [jnp.max(x, axis=-1, keepdims=True) for x in sc]
        pr = [jnp.exp(x - m).astype(BF16) for x, m in zip(sc, mx)]
        ov = [jnp.dot(p, v, preferred_element_type=F32) for p, v in zip(pr, vb)]
        mb = [jnp.broadcast_to(m, (blk, LANES)) for m in mx]
        if first_pattern:
            new = [(m, o[:, LANES:], o[:, :LANES]) for m, o in zip(mb, ov)]
        else:
            old = [(load(m_s, kind, s), load(l_s, kind, s), load(acc_s, kind, s)) for s in starts]
            new = []
            for (m_old, l_old, acc_old), m, o in zip(old, mb, ov):
                m_new = jnp.maximum(m_old, m)
                a_old = jnp.exp(m_old - m_new)
                a_new = jnp.exp(m - m_new)
                new.append((m_new, a_old * l_old + a_new * o[:, LANES:],
                            a_old * acc_old + a_new * o[:, :LANES]))
        for s, (m_new, l_new, acc_new) in zip(starts, new):
            store(m_s, kind, s, m_new)
            store(l_s, kind, s, l_new)
            store(acc_s, kind, s, acc_new)

    def loop(n_iter, kind, starts_of, with_prev, first_pattern):
        def body(i, carry):
            group(kind, starts_of(i), with_prev, first_pattern)
            return carry
        lax.fori_loop(0, n_iter, body, 0)

    nb4 = qseg // blk
    group("c", [r * qseg for r in range(4)], False, True)
    loop(2, "c", lambda i: [(2 * i + j) * qseg + n * blk for j in range(2) for n in range(1, nb4)], True, True)
    loop(2, "s", lambda i: [(2 * i + j) * qseg + a for j in range(2) for a in range(4)], False, False)
    group("q", [0], False, False)
    loop((seq // blk - 1) // ATTN_GROUP, "q",
         lambda i: [(1 + ATTN_GROUP * i + j) * slab for j in range(ATTN_GROUP)], True, False)

    tmp[...] = acc_s[...] / l_s[...]
    for r in range(4):
        qs[pl.ds(r, qseg, stride=4), :] = tmp[r * qseg:(r + 1) * qseg, :]
    o_ref[...] = qs[...].astype(BF16)


def _attention(proj, pos3, freq, qw, kw, batch, seq):
    t = proj.shape[0]
    hd = HEAD_DIM
    nh = N_ATTN_HEADS
    assert seq % (ATTN_BLOCK * max(DILATIONS)) == 0 and (seq // ATTN_BLOCK - 1) % ATTN_GROUP == 0
    scr = pltpu.VMEM((seq, hd), F32)
    return pl.pallas_call(
        _attn_kernel,
        grid=(batch, nh),
        in_specs=[pl.BlockSpec((seq, hd), lambda b, h: (b, h)),
                  pl.BlockSpec((seq, hd), lambda b, h: (b, nh + h)),
                  pl.BlockSpec((seq, hd), lambda b, h: (b, 2 * nh + h)),
                  pl.BlockSpec((1, seq, 1), lambda b, h: (b, 0, 0)),
                  pl.BlockSpec((1, hd), lambda b, h: (0, 0)),
                  pl.BlockSpec((1, hd), lambda b, h: (0, 0)),
                  pl.BlockSpec((1, hd), lambda b, h: (0, 0))],
        out_specs=pl.BlockSpec((seq, hd), lambda b, h: (b, h)),
        out_shape=jax.ShapeDtypeStruct((t, D_ATTN), BF16),
        scratch_shapes=[scr] * 9,
        compiler_params=_cparams(("parallel", "arbitrary")),
        name="attention",
    )(proj, proj, proj, pos3, freq, qw, kw)


def _split3(v):
    hi = v.astype(BF16)
    r1 = v - hi.astype(F32)
    mid = r1.astype(BF16)
    lo = (r1 - mid.astype(F32)).astype(BF16)
    return hi, mid, lo


def _ssd_kernel(xs_ref, bm_ref, cm_ref, z_ref, dtr_ref, cwx_ref, cwb_ref, cwc_ref,
                cbx_ref, cbb_ref, cbc_ref, dtb_ref, alog_ref, dsk_ref, nw_ref, o_ref,
                xpad, x_s, b_s, c_s, dt_s, da_s, dtb_s, daf_s):
    seq = xs_ref.shape[0]
    ck = SSD_CHUNK
    g = pl.program_id(1)
    pad = SUBLANES

    def conv_silu(src_ref, w_ref, b_ref, dst):
        xpad[0:pad, :] = jnp.zeros((pad, LANES), F32)
        xpad[pad:, :] = src_ref[...].astype(F32)
        y = b_ref[...] + jnp.zeros((seq, LANES), F32)
        for j in range(CONV_WIDTH):
            y = y + w_ref[j:j + 1, :] * xpad[pl.ds(pad - (CONV_WIDTH - 1) + j, seq), :]
        dst[...] = y * jax.nn.sigmoid(y)

    conv_silu(xs_ref, cwx_ref, cbx_ref, x_s)
    conv_silu(bm_ref, cwb_ref, cbb_ref, b_s)
    conv_silu(cm_ref, cwc_ref, cbc_ref, c_s)

    @pl.when(g == 0)
    def _():
        xr = dtr_ref[...] + dtb_ref[...]
        dt = jnp.maximum(xr, 0.0) + jnp.log1p(jnp.exp(-jnp.abs(xr)))
        for i, piece in enumerate(_split3(dt)):
            dt_s[i] = piece
        for i, piece in enumerate(_split3(dt * (-jnp.exp(alog_ref[...])))):
            da_s[i] = piece

    def select(v_s, i, sel):
        return jnp.dot(v_s[i], sel, preferred_element_type=F32)

    hrow = lax.broadcasted_iota(jnp.int32, (LANES, 2 * LANES), 0)
    hlane = lax.broadcasted_iota(jnp.int32, (LANES, 2 * LANES), 1)
    brow = lax.broadcasted_iota(jnp.int32, (LANES, LANES), 0)
    blane = lax.broadcasted_iota(jnp.int32, (LANES, LANES), 1)
    sel_b = (brow == 2 * g + (blane >= SSM_HEAD_DIM).astype(jnp.int32)).astype(BF16)
    sel_f = (hrow == 2 * g + (hlane >= LANES).astype(jnp.int32)).astype(BF16)
    dtb_s[...] = select(dt_s, 0, sel_b) + select(dt_s, 1, sel_b) + select(dt_s, 2, sel_b)
    for i in range(3):
        daf_s[i] = select(da_s, i, sel_f).astype(BF16)

    li = lax.broadcasted_iota(jnp.int32, (ck, ck), 0)
    si = lax.broadcasted_iota(jnp.int32, (ck, ck), 1)
    causal = si <= li
    eye = si == li
    tril_bf = causal.astype(BF16)
    head1 = lax.broadcasted_iota(jnp.int32, (ck, LANES), 1) >= SSM_HEAD_DIM
    head1_row = lax.broadcasted_iota(jnp.int32, (1, LANES), 1) >= SSM_HEAD_DIM

    def chunk(c, prev):
        r = pl.ds(pl.multiple_of(c * ck, ck), ck)
        xc = x_s[r, :]
        bc = b_s[r, :].astype(BF16)
        cc = c_s[r, :].astype(BF16)
        acs2 = (jnp.dot(tril_bf, daf_s[0, r, :], preferred_element_type=F32)
                + jnp.dot(tril_bf, daf_s[1, r, :], preferred_element_type=F32)
                + jnp.dot(tril_bf, daf_s[2, r, :], preferred_element_type=F32))
        acs_f = (acs2[:, :LANES], acs2[:, LANES:])
        acs_b = jnp.where(head1, acs_f[1], acs_f[0])
        last_b = jnp.where(head1_row, acs_f[1][ck - 1:ck, :], acs_f[0][ck - 1:ck, :])
        cb = lax.dot_general(cc, bc, (((1,), (1,)), ((), ())), preferred_element_type=F32)
        xdt = xc * dtb_s[r, :]
        scores = []
        for k in range(2):
            acs_row = jnp.sum(jnp.where(eye, acs_f[k], 0.0), axis=0, keepdims=True)
            decay = jnp.exp(jnp.where(causal, acs_f[k] - acs_row, -jnp.inf))
            scores.append((cb * decay).astype(BF16))
        xdt2 = jnp.concatenate([jnp.where(head1, 0.0, xdt).astype(BF16),
                                jnp.where(head1, xdt, 0.0).astype(BF16)], axis=0)
        y_diag = jnp.dot(jnp.concatenate(scores, axis=1), xdt2, preferred_element_type=F32)
        y_off = jnp.dot(cc, prev.astype(BF16), preferred_element_type=F32) * jnp.exp(acs_b)
        y = y_diag + y_off + dsk_ref[...] * xc
        zc = z_ref[r, :].astype(F32)
        y = y * (zc * jax.nn.sigmoid(zc))
        y = y * lax.rsqrt(jnp.mean(y * y, axis=-1, keepdims=True) + EPS) * nw_ref[...]
        o_ref[r, :] = y.astype(BF16)
        w = (xdt * jnp.exp(last_b - acs_b)).astype(BF16)
        st = lax.dot_general(bc, w, (((0,), (0,)), ((), ())), preferred_element_type=F32)
        return prev * jnp.exp(last_b) + st

    lax.fori_loop(0, seq // ck, chunk, jnp.zeros((SSM_STATE, LANES), F32), unroll=8)


def _ssd(proj, dt_raw, conv_w, conv_b, dtb, alog, dsk, nw, batch, seq):
    t = proj.shape[0]
    gs = SSM_GROUPS
    col = lambda off: (lambda b, g: (b, off + g))
    wcol = lambda off: (lambda b, g: (0, off + g))
    zq, xq, bq, cq = 3 * D_ATTN // LANES, 4 * D_ATTN // LANES, 5 * D_ATTN // LANES, 6 * D_ATTN // LANES
    const = lambda b, g: (0, 0)
    scr = pltpu.VMEM((seq, LANES), F32)
    return pl.pallas_call(
        _ssd_kernel,
        grid=(batch, gs),
        in_specs=[pl.BlockSpec((seq, LANES), col(xq)),
                  pl.BlockSpec((seq, LANES), col(bq)),
                  pl.BlockSpec((seq, LANES), col(cq)),
                  pl.BlockSpec((seq, LANES), col(zq)),
                  pl.BlockSpec((seq, LANES), lambda b, g: (b, 0)),
                  pl.BlockSpec((CONV_WIDTH, LANES), wcol(0)),
                  pl.BlockSpec((CONV_WIDTH, LANES), wcol(gs)),
                  pl.BlockSpec((CONV_WIDTH, LANES), wcol(2 * gs)),
                  pl.BlockSpec((1, LANES), wcol(0)),
                  pl.BlockSpec((1, LANES), wcol(gs)),
                  pl.BlockSpec((1, LANES), wcol(2 * gs)),
                  pl.BlockSpec((1, LANES), const),
                  pl.BlockSpec((1, LANES), const),
                  pl.BlockSpec((1, LANES), lambda b, g: (0, g)),
                  pl.BlockSpec((1, LANES), lambda b, g: (0, g))],
        out_specs=pl.BlockSpec((seq, LANES), lambda b, g: (b, g)),
        out_shape=jax.ShapeDtypeStruct((t, D_SSM), BF16),
        scratch_shapes=[pltpu.VMEM((seq + SUBLANES, LANES), F32), scr, scr, scr,
                        pltpu.VMEM((3, seq, LANES), BF16), pltpu.VMEM((3, seq, LANES), BF16), scr,
                        pltpu.VMEM((3, seq, 2 * LANES), BF16)],
        compiler_params=_cparams(("arbitrary", "arbitrary")),
        name="ssd",
    )(proj, proj, proj, proj, dt_raw, conv_w, conv_w, conv_w, conv_b, conv_b, conv_b,
      dtb, alog, dsk, nw)


def _out_router_kernel(ya_ref, ys_ref, wa_ref, ws_ref, x_ref, mod_ref, nw_ref, rw_ref, rb_ref,
                       x1_ref, h2_ref, eid_ref, ew_ref, rank_ref, cnt_ref, cnt_s):
    tm = x_ref.shape[0]
    parts = []
    for r0 in range(0, tm, OUT_SUB):
        rows = slice(r0, r0 + OUT_SUB)
        mix = (jnp.dot(ya_ref[rows, :], wa_ref[...], preferred_element_type=F32)
               + jnp.dot(ys_ref[rows, :], ws_ref[...], preferred_element_type=F32))
        x1 = x_ref[rows, :] + mod_ref[0, 2:3, :] * mix
        x1_ref[rows, :] = x1
        part = _modulated_norm(x1, nw_ref[...], mod_ref[0, 3:4, :], mod_ref[0, 4:5, :])
        h2_ref[rows, :] = _pack_bf16_pair(part)
        parts.append(part)
    h2 = jnp.concatenate(parts, axis=0)

    h_hi = h2.astype(BF16)
    h_lo = (h2 - h_hi.astype(F32)).astype(BF16)
    rw = rw_ref[...]
    w_hi = rw.astype(BF16)
    w_lo = (rw - w_hi.astype(F32)).astype(BF16)
    nt = (((1,), (1,)), ((), ()))
    logits = (lax.dot_general(w_hi, h_hi, nt, preferred_element_type=F32)
              + lax.dot_general(w_hi, h_lo, nt, preferred_element_type=F32)
              + lax.dot_general(w_lo, h_hi, nt, preferred_element_type=F32)) + rb_ref[...]

    ng, ne = N_EXPERT_GROUPS, EXPERTS_PER_GROUP
    iota = lax.broadcasted_iota(jnp.int32, (ne, tm), 0)

    def argmax0(v):
        mx = jnp.max(v, axis=0, keepdims=True)
        idx = jnp.min(jnp.where(v == mx, iota, ne), axis=0, keepdims=True)
        return mx, idx

    gl = logits[N_EXPERTS:N_EXPERTS + ng, :]
    gmax, gidx = argmax0(gl)
    g_gate = 1.0 / jnp.sum(jnp.exp(gl - gmax), axis=0, keepdims=True)
    e_in = jnp.zeros((ne, tm), F32)
    for gi in range(ng):
        e_in = e_in + jnp.where(gidx == gi, logits[gi * ne:(gi + 1) * ne, :], 0.0)
    m1, i1 = argmax0(e_in)
    m2, i2 = argmax0(jnp.where(iota == i1, -jnp.inf, e_in))
    p2 = jnp.exp(m2 - m1)
    den = 1.0 + p2
    e0 = gidx * ne + i1
    e1 = gidx * ne + i2
    eid_ref[0, 0:1, :] = e0
    eid_ref[0, 1:2, :] = e1
    ew_ref[0:1, :] = (1.0 / den) * g_gate
    ew_ref[1:2, :] = (p2 / den) * g_gate

    @pl.when(pl.program_id(0) == 0)
    def _():
        cnt_s[...] = jnp.zeros(cnt_s.shape, F32)

    eiota = lax.broadcasted_iota(jnp.int32, (N_EXPERTS, tm), 0)
    oh0 = eiota == e0
    oh1 = eiota == e1
    both = jnp.logical_or(oh0, oh1)
    ti = lax.broadcasted_iota(jnp.int32, (tm, tm), 0)
    tj = lax.broadcasted_iota(jnp.int32, (tm, tm), 1)
    earlier = (ti < tj).astype(BF16)
    before = cnt_s[...] + jnp.dot(both.astype(BF16), earlier, preferred_element_type=F32)
    rank_ref[0, 0:1, :] = jnp.sum(jnp.where(oh0, before, 0.0), axis=0, keepdims=True).astype(jnp.int32)
    rank_ref[0, 1:2, :] = jnp.sum(jnp.where(oh1, before, 0.0), axis=0, keepdims=True).astype(jnp.int32)
    total = cnt_s[...] + jnp.sum(both.astype(F32), axis=1, keepdims=True)
    cnt_s[...] = total
    cnt_ref[...] = total.astype(jnp.int32)


def _out_router(ya, ys, wa, ws, x2, mod3, norm_w, rw, rb, seq):
    t, d = x2.shape
    tm = OUT_TM
    const = lambda i: (0, 0)
    return pl.pallas_call(
        _out_router_kernel,
        grid=(t // tm,),
        in_specs=[pl.BlockSpec((tm, D_ATTN), lambda i: (i, 0)),
                  pl.BlockSpec((tm, D_SSM), lambda i: (i, 0)),
                  pl.BlockSpec((D_ATTN, d), const, pipeline_mode=pl.Buffered(1)),
                  pl.BlockSpec((D_SSM, d), const, pipeline_mode=pl.Buffered(1)),
                  pl.BlockSpec((tm, d), lambda i: (i, 0)),
                  pl.BlockSpec((1, 6, d), lambda i: (i * tm // seq, 0, 0)),
                  pl.BlockSpec((1, d), const),
                  pl.BlockSpec((LANES, d), const, pipeline_mode=pl.Buffered(1)),
                  pl.BlockSpec((LANES, 1), const)],
        out_specs=[pl.BlockSpec((tm, d), lambda i: (i, 0)),
                   pl.BlockSpec((tm, d // 2), lambda i: (i, 0)),
                   pl.BlockSpec((1, 2, tm), lambda i: (i, 0, 0)),
                   pl.BlockSpec((2, tm), lambda i: (0, i)),
                   pl.BlockSpec((1, 2, tm), lambda i: (i, 0, 0)),
                   pl.BlockSpec((N_EXPERTS, 1), const)],
        out_shape=[jax.ShapeDtypeStruct((t, d), F32),
                   jax.ShapeDtypeStruct((t, d // 2), jnp.uint32),
                   jax.ShapeDtypeStruct((t // tm, 2, tm), jnp.int32),
                   jax.ShapeDtypeStruct((2, t), F32),
                   jax.ShapeDtypeStruct((t // tm, 2, tm), jnp.int32),
                   jax.ShapeDtypeStruct((N_EXPERTS, 1), jnp.int32)],
        scratch_shapes=[pltpu.VMEM((N_EXPERTS, 1), F32)],
        compiler_params=_cparams(("arbitrary",)),
        name="out_router",
    )(ya, ys, wa, ws, x2, mod3, norm_w, rw, rb)


def _for_each_assignment(tm, body):
    def group(gi, carry):
        for jj in range(SUBLANES):
            for k in range(2):
                body(k, gi, jj, k * tm + gi * SUBLANES + jj)
        return carry
    lax.fori_loop(0, tm // SUBLANES, group, 0)


def _rows3(a):
    return a.reshape(a.shape[0] // SUBLANES, SUBLANES, a.shape[1])


def _scatter_kernel(zstart_ref, dest_ref, h_ref, xd_hbm, sem, zbuf, zsem):
    tm = h_ref.shape[0] * SUBLANES

    @pl.when(pl.program_id(0) == 0)
    def _():
        zbuf[...] = jnp.zeros(zbuf.shape, zbuf.dtype)

        def zero_copy(j):
            start = pl.multiple_of(jnp.maximum(zstart_ref[j], 0), MOE_ROWS)
            return pltpu.make_async_copy(zbuf, xd_hbm.at[pl.ds(start, MOE_ROWS)], zsem)

        def z_issue(j, carry):
            @pl.when(zstart_ref[j] >= 0)
            def _():
                zero_copy(j).start()
            return carry

        def z_wait(j, carry):
            @pl.when(zstart_ref[j] >= 0)
            def _():
                zero_copy(j).wait()
            return carry

        lax.fori_loop(0, zstart_ref.shape[0], z_issue, 0)
        lax.fori_loop(0, zstart_ref.shape[0], z_wait, 0)

    def issue(k, gi, jj, j):
        pltpu.make_async_copy(h_ref.at[gi, pl.ds(jj, 1)], xd_hbm.at[pl.ds(dest_ref[0, 0, j], 1)], sem).start()

    _for_each_assignment(tm, issue)
    for _ in range(2 * SUBLANES):
        pltpu.make_async_copy(h_ref.at[:, 0], xd_hbm.at[pl.ds(0, tm // SUBLANES)], sem).wait()


def _scatter_rows(zstart, dest3, h2, n_rows):
    t, d = h2.shape
    n_tiles = dest3.shape[0]
    tm = dest3.shape[2] // 2
    grid_spec = pltpu.PrefetchScalarGridSpec(
        num_scalar_prefetch=1,
        grid=(n_tiles,),
        in_specs=[pl.BlockSpec((1, 1, 2 * tm), lambda i, zs: (i, 0, 0), memory_space=pltpu.SMEM),
                  pl.BlockSpec((tm // SUBLANES, SUBLANES, d), lambda i, zs: (i, 0, 0))],
        out_specs=pl.BlockSpec(memory_space=pl.ANY),
        scratch_shapes=[pltpu.SemaphoreType.DMA(()), pltpu.VMEM((MOE_ROWS, d), h2.dtype),
                        pltpu.SemaphoreType.DMA(())],
    )
    return pl.pallas_call(
        _scatter_kernel,
        grid_spec=grid_spec,
        out_shape=jax.ShapeDtypeStruct((n_rows, d), h2.dtype),
        compiler_params=_cparams(("arbitrary",)),
        name="scatter_rows",
    )(zstart, dest3, _rows3(h2))


def _experts_kernel(be_ref, nv_ref, first_ref, nxt_ref, slot_ref, x_ref, wg_hbm, wu_hbm, wd_hbm, o_ref,
                    wg_f, wu_f, wd_f, sems, wg_s, wu_s, wd_s):
    i = pl.program_id(0)
    n_valid = nv_ref[i]
    half = x_ref.shape[1]

    def weight_copies(e, slot):
        return (pltpu.make_async_copy(wg_hbm.at[e], wg_f.at[slot], sems.at[slot, 0]),
                pltpu.make_async_copy(wu_hbm.at[e], wu_f.at[slot], sems.at[slot, 1]),
                pltpu.make_async_copy(wd_hbm.at[e], wd_f.at[slot], sems.at[slot, 2]))

    @pl.when(i == 0)
    def _():
        for c in weight_copies(be_ref[0], slot_ref[0]):
            c.start()

    @pl.when(first_ref[i] == 1)
    def _():
        slot = slot_ref[i]
        for c in weight_copies(be_ref[i], slot):
            c.wait()

        @pl.when(nxt_ref[i] >= 0)
        def _():
            for c in weight_copies(nxt_ref[i], 1 - slot):
                c.start()

        wg_s[...] = wg_f[slot].astype(BF16)
        wu_s[...] = wu_f[slot].astype(BF16)
        wd_s[...] = wd_f[slot].astype(BF16)

    @pl.when(n_valid > 0)
    def _():
        x_lo, x_hi = _unpack_bf16_pair(x_ref[...])
        x_lo = x_lo.astype(BF16)
        x_hi = x_hi.astype(BF16)
        gt = (jnp.dot(x_lo, wg_s[0:half, :], preferred_element_type=F32)
              + jnp.dot(x_hi, wg_s[half:, :], preferred_element_type=F32))
        up = (jnp.dot(x_lo, wu_s[0:half, :], preferred_element_type=F32)
              + jnp.dot(x_hi, wu_s[half:, :], preferred_element_type=F32))
        hid = (gt * jax.nn.sigmoid(gt)) * up
        o_ref[...] = _pack_bf16_pair(jnp.dot(hid.astype(BF16), wd_s[...], preferred_element_type=F32))

    @pl.when(n_valid == 0)
    def _():
        o_ref[...] = jnp.zeros(o_ref.shape, o_ref.dtype)


def _experts(block_e, n_valid, first, nxt, slot, xd, w_gate, w_up, w_down):
    rows = MOE_ROWS
    n_blocks = xd.shape[0] // rows
    half = xd.shape[1]
    d = 2 * half
    de = w_gate.shape[2]
    blk = lambda i, *_: (i, 0)
    grid_spec = pltpu.PrefetchScalarGridSpec(
        num_scalar_prefetch=5,
        grid=(n_blocks,),
        in_specs=[pl.BlockSpec((rows, half), blk),
                  pl.BlockSpec(memory_space=pl.ANY),
                  pl.BlockSpec(memory_space=pl.ANY),
                  pl.BlockSpec(memory_space=pl.ANY)],
        out_specs=pl.BlockSpec((rows, half), blk),
        scratch_shapes=[pltpu.VMEM((2, d, de), F32),
                        pltpu.VMEM((2, d, de), F32),
                        pltpu.VMEM((2, de, d), F32),
                        pltpu.SemaphoreType.DMA((2, 3)),
                        pltpu.VMEM((d, de), BF16),
                        pltpu.VMEM((d, de), BF16),
                        pltpu.VMEM((de, d), BF16)],
    )
    return pl.pallas_call(
        _experts_kernel,
        grid_spec=grid_spec,
        out_shape=jax.ShapeDtypeStruct((n_blocks * rows, half), jnp.uint32),
        compiler_params=_cparams(("arbitrary",)),
        name="experts",
    )(block_e, n_valid, first, nxt, slot, xd, w_gate, w_up, w_down)


def _combine_kernel(dest_ref, destn_ref, yd_hbm, x1_ref, mod_ref, ew_ref, o_ref, ybuf, sems):
    i = pl.program_id(0)
    n = pl.num_programs(0)
    tm = x1_ref.shape[0]
    tg = tm // SUBLANES
    slot = i % 2

    def gather(d_ref, s):
        def issue(k, gi, jj, j):
            pltpu.make_async_copy(yd_hbm.at[pl.ds(d_ref[0, 0, j], 1)],
                                  ybuf.at[s, k * tg + gi, pl.ds(jj, 1)], sems.at[s]).start()
        _for_each_assignment(tm, issue)

    @pl.when(i == 0)
    def _():
        gather(dest_ref, 0)

    @pl.when(i + 1 < n)
    def _():
        gather(destn_ref, 1 - slot)

    for _ in range(SUBLANES):
        pltpu.make_async_copy(yd_hbm.at[pl.ds(0, 2 * tg)], ybuf.at[slot, :, 0], sems.at[slot]).wait()
    half = ybuf.shape[3]
    y0_lo, y0_hi = _unpack_bf16_pair(ybuf[slot, 0:tg].reshape(tm, half))
    y1_lo, y1_hi = _unpack_bf16_pair(ybuf[slot, tg:2 * tg].reshape(tm, half))
    w0 = ew_ref[:, 0:1]
    w1 = ew_ref[:, 1:2]
    o_ref[:, 0:half] = x1_ref[:, 0:half] + mod_ref[0, 5:6, 0:half] * (w0 * y0_lo + w1 * y1_lo)
    o_ref[:, half:] = x1_ref[:, half:] + mod_ref[0, 5:6, half:] * (w0 * y0_hi + w1 * y1_hi)


def _combine(dest3, yd, x1, mod3, ew_t, seq):
    t, d = x1.shape
    n = dest3.shape[0]
    tm = dest3.shape[2] // 2
    smem = lambda im: pl.BlockSpec((1, 1, 2 * tm), im, memory_space=pltpu.SMEM)
    return pl.pallas_call(
        _combine_kernel,
        grid=(n,),
        in_specs=[smem(lambda i: (i, 0, 0)), smem(lambda i: (jnp.minimum(i + 1, n - 1), 0, 0)),
                  pl.BlockSpec(memory_space=pl.ANY),
                  pl.BlockSpec((tm, d), lambda i: (i, 0)),
                  pl.BlockSpec((1, 6, d), lambda i: (i * tm // seq, 0, 0)),
                  pl.BlockSpec((tm, 2), lambda i: (i, 0))],
        out_specs=pl.BlockSpec((tm, d), lambda i: (i, 0)),
        out_shape=jax.ShapeDtypeStruct((t, d), F32),
        scratch_shapes=[pltpu.VMEM((2, 2 * tm // SUBLANES, SUBLANES, yd.shape[1]), yd.dtype),
                        pltpu.SemaphoreType.DMA((2,))],
        compiler_params=_cparams(("arbitrary",)),
        name="combine",
    )(dest3, dest3, yd, x1, mod3, ew_t)


def _layout(counts, n_assign):
    rows = MOE_ROWS
    n_blocks = (n_assign + N_EXPERTS * (rows - 1)) // rows
    padded = (counts + rows - 1) // rows * rows
    pend = jnp.cumsum(padded)
    pstart = pend - padded
    bstart = jnp.arange(n_blocks, dtype=jnp.int32) * rows
    block_e = jnp.minimum(jnp.sum((pend[None, :] <= bstart[:, None]).astype(jnp.int32), axis=1),
                          N_EXPERTS - 1)
    n_valid = jnp.clip(pstart[block_e] + counts[block_e] - bstart, 0, rows)
    n_valid = jnp.where(bstart < pend[-1], n_valid, 0)
    min_used = n_assign // rows
    tail = pend[-1] + jnp.arange(n_blocks - min_used, dtype=jnp.int32) * rows
    zstart = jnp.concatenate([jnp.where(counts > 0, pend - rows, -1),
                              jnp.where(tail < n_blocks * rows, tail, -1)])
    first = (n_valid > 0) & jnp.concatenate([jnp.ones((1,), bool), block_e[1:] != block_e[:-1]])
    slot = (jnp.cumsum(first.astype(jnp.int32)) - 1) % 2
    ids = jnp.where(counts > 0, jnp.arange(N_EXPERTS, dtype=jnp.int32), N_EXPERTS)
    later = jnp.concatenate([lax.cummin(ids[::-1])[::-1][1:], jnp.full((1,), N_EXPERTS, jnp.int32)])
    nxt = jnp.where(later < N_EXPERTS, later, -1)[block_e]
    sched = (first.astype(jnp.int32), nxt.astype(jnp.int32), jnp.maximum(slot, 0).astype(jnp.int32))
    return (pstart.astype(jnp.int32), zstart.astype(jnp.int32), block_e.astype(jnp.int32),
            n_valid.astype(jnp.int32), sched, n_blocks * rows)


def kernel(x, c, positions, ada_w, ada_b, norm1_w, w_in, q_norm_w, k_norm_w, conv_w, conv_b,
           dt_bias, a_log, d_skip, ssm_norm_w, w_out, norm2_w, router_group_w, router_group_b,
           router_expert_w, router_expert_b, w_gate, w_up, w_down):
    batch, seq, d = x.shape
    t = batch * seq
    depth = ada_w.shape[0]
    nh_pad = LANES - N_SSM_HEADS
    inv_freq = ROPE_THETA ** (-(jnp.arange(0, ROT_DIM, 2, dtype=F32) / ROT_DIM))
    freq = jnp.concatenate([inv_freq, inv_freq, jnp.zeros((HEAD_DIM - ROT_DIM,), F32)]).reshape(1, HEAD_DIM)
    pos3 = positions.reshape(batch, seq, 1)
    xc = x.reshape(t, d)
    for l in range(depth):
        mod3 = _adaln(c, ada_w[l], ada_b[l]).reshape(batch, 6, d)
        w_main = w_in[l][:, :N_MAIN].astype(BF16)
        w_dt = jnp.pad(w_in[l][:, N_MAIN:], ((0, 0), (0, nh_pad))).astype(BF16)
        proj, dt_raw = _in_proj(xc, mod3, norm1_w[l].reshape(1, d), w_main, w_dt, seq)
        y_attn = _attention(proj, pos3, freq, q_norm_w[l].reshape(1, HEAD_DIM),
                            k_norm_w[l].reshape(1, HEAD_DIM), batch, seq)
        y_ssm = _ssd(proj, dt_raw, conv_w[l], conv_b[l].reshape(1, -1),
                     jnp.pad(dt_bias[l], (0, nh_pad)).reshape(1, LANES),
                     jnp.pad(a_log[l], (0, nh_pad)).reshape(1, LANES),
                     jnp.repeat(d_skip[l], SSM_HEAD_DIM).reshape(1, D_SSM),
                     ssm_norm_w[l].reshape(1, D_SSM), batch, seq)
        rw = jnp.concatenate([router_expert_w[l].T, router_group_w[l].T,
                              jnp.zeros((LANES - N_EXPERTS - N_EXPERT_GROUPS, d), F32)])
        rb = jnp.concatenate([router_expert_b[l], router_group_b[l],
                              jnp.zeros((LANES - N_EXPERTS - N_EXPERT_GROUPS,), F32)]).reshape(LANES, 1)
        wo = w_out[l].astype(BF16)
        x1, h2, eid3, ew, rank3, counts = _out_router(y_attn, y_ssm, wo[:D_ATTN], wo[D_ATTN:], xc, mod3,
                                                      norm2_w[l].reshape(1, d), rw, rb, seq)
        pstart, zstart, block_e, n_valid, sched, n_rows = _layout(counts.reshape(N_EXPERTS), 2 * t)
        seg = jnp.sum(jnp.where(eid3[..., None] == jnp.arange(N_EXPERTS, dtype=jnp.int32), pstart, 0), axis=-1)
        dest3 = (seg + rank3).reshape(t // OUT_TM, 1, 2 * OUT_TM)
        xd = _scatter_rows(zstart, dest3, h2, n_rows)
        yd = _experts(block_e, n_valid, *sched, xd, w_gate[l], w_up[l], w_down[l])
        xc = _combine(dest3, yd, x1, mod3, ew.T, seq)
    return xc.reshape(batch, seq, d)
```
